```python
import jax, jax.numpy as jnp
from jax import lax
import numpy as np

D_MODEL = 1024
BATCH = 4
SEQ = 4096
DEPTH = 4

N_MIXERS = 4
EPS = 1e-6

D_FF = 2816
FFN_CONV = 3

D_RNN = 1280
LRU_BLOCKS = 10
LRU_BLK = D_RNN // LRU_BLOCKS
LRU_CONV = 4
LRU_C = 8.0

D_INNER = 2 * D_MODEL
SSD_HEAD_DIM = 64
SSD_HEADS = D_INNER // SSD_HEAD_DIM
SSD_GROUPS = 4
SSD_HPG = SSD_HEADS // SSD_GROUPS
SSD_STATE = 128
SSD_CONV = 4
SSD_CHUNK = 128
SSD_GN = SSD_GROUPS * SSD_STATE
SSD_CONV_DIM = D_INNER + 2 * SSD_GN
SSD_IN = D_INNER + SSD_CONV_DIM + SSD_HEADS

SWA_HEAD_DIM = 64
SWA_Q_HEADS = D_MODEL // SWA_HEAD_DIM
SWA_KV_HEADS = 4
SWA_GQA = SWA_Q_HEADS // SWA_KV_HEADS
SWA_IN = (SWA_Q_HEADS + 2 * SWA_KV_HEADS) * SWA_HEAD_DIM
WINDOW = 128
SWA_BLOCK = 128
ROPE_THETA = 500000.0
ROT_DIM = SWA_HEAD_DIM // 4

HGRN_EXPAND = 128
HGRN_HEADS = D_MODEL // HGRN_EXPAND
HGRN_DK = HGRN_EXPAND
HGRN_DV = D_MODEL // HGRN_HEADS
HGRN_FD = HGRN_HEADS * HGRN_DK
HGRN_IN = 2 * HGRN_FD + 2 * D_MODEL
HGRN_CHUNK = 64

N_LRU = (DEPTH + 3) // 4
N_SSD = (DEPTH + 2) // 4
N_SWA = (DEPTH + 1) // 4
N_HGRN = DEPTH // 4

kernel_name = "hybrid_interleaved_rglru_ssd_swa_hgrn2"


def _rmsnorm(x, g):
    xf = x.astype(jnp.float32)
    y = xf * lax.rsqrt(jnp.mean(xf * xf, axis=-1, keepdims=True) + EPS)
    return (y * g.astype(jnp.float32)).astype(x.dtype)


def _causal_dwconv(x, w, b):
    k_w, ch = w.shape
    y = lax.conv_general_dilated(x, w.reshape(k_w, 1, ch).astype(x.dtype), window_strides=(1,),
                                 padding=[(k_w - 1, 0)], dimension_numbers=('NWC', 'WIO', 'NWC'),
                                 feature_group_count=ch)
    return y + b.astype(x.dtype)


def _linear_combine(left, right):
    a1, b1 = left
    a2, b2 = right
    return a1 * a2, a2 * b1 + b2


def _rg_lru_mixer(h, w_in, conv_w, conv_b, w_gate, b_gate, lam, w_out):
    bsz, s, _ = h.shape
    proj = h @ w_in
    gate_branch, xr = proj[..., :D_RNN], proj[..., D_RNN:]
    xc = _causal_dwconv(xr, conv_w, conv_b)
    xb = xc.reshape(bsz, s, LRU_BLOCKS, LRU_BLK)
    gates = (jnp.einsum('bsnc,nce->bsne', xb, w_gate) + b_gate).astype(jnp.float32)
    r = jax.nn.sigmoid(gates[..., :LRU_BLK]).reshape(bsz, s, D_RNN)
    i = jax.nn.sigmoid(gates[..., LRU_BLK:]).reshape(bsz, s, D_RNN)
    log_a = -LRU_C * r * jax.nn.softplus(-lam.astype(jnp.float32))
    a = jnp.exp(log_a)
    u = jnp.sqrt(-jnp.expm1(2.0 * log_a)) * (i * xc.astype(jnp.float32))
    _, hs = lax.associative_scan(_linear_combine, (a, u), axis=1)
    y = jax.nn.gelu(gate_branch.astype(jnp.float32)) * hs
    return y.astype(h.dtype) @ w_out


def _ssd_mixer(h, w_in, conv_w, conv_b, dt_bias, a_log, d_skip, norm_w, w_out):
    bsz, s, _ = h.shape
    nc, L, G, R, P, N = s // SSD_CHUNK, SSD_CHUNK, SSD_GROUPS, SSD_HPG, SSD_HEAD_DIM, SSD_STATE
    proj = h @ w_in
    z = proj[..., :D_INNER]
    xbc = proj[..., D_INNER:D_INNER + SSD_CONV_DIM]
    dt_raw = proj[..., D_INNER + SSD_CONV_DIM:]
    xbc = jax.nn.silu(_causal_dwconv(xbc, conv_w, conv_b)).astype(jnp.float32)
    xs = xbc[..., :D_INNER]
    bm = xbc[..., D_INNER:D_INNER + SSD_GN]
    cm = xbc[..., D_INNER + SSD_GN:]
    dt = jax.nn.softplus(dt_raw.astype(jnp.float32) + dt_bias.astype(jnp.float32))
    a = -jnp.exp(a_log.astype(jnp.float32)).reshape(G, R)
    x_c = xs.reshape(bsz, nc, L, G, R, P)
    dt_c = dt.reshape(bsz, nc, L, G, R)
    b_c = bm.reshape(bsz, nc, L, G, N)
    c_c = cm.reshape(bsz, nc, L, G, N)
    x_dt = x_c * dt_c[..., None]
    cs = jnp.cumsum(dt_c * a, axis=2)
    causal = jnp.tril(jnp.ones((L, L), dtype=bool))[None, None, :, :, None, None]
    seg = cs[:, :, :, None] - cs[:, :, None, :]
    decay = jnp.exp(jnp.where(causal, seg, -jnp.inf))
    cb = jnp.einsum('bclgn,bcsgn->bclsg', c_c, b_c)
    y_diag = jnp.einsum('bclsgr,bcsgrp->bclgrp', cb[..., None] * decay, x_dt)
    decay_to_end = jnp.exp(cs[:, :, -1:] - cs)
    states = jnp.einsum('bclgn,bclgr,bclgrp->bcgrpn', b_c, decay_to_end, x_dt)
    chunk_decay = jnp.exp(cs[:, :, -1])

    def step(carry, inp):
        st, dec = inp
        return carry * dec[..., None, None] + st, carry

    init = jnp.zeros((bsz, G, R, P, N), jnp.float32)
    _, prev = lax.scan(step, init, (jnp.moveaxis(states, 1, 0), jnp.moveaxis(chunk_decay, 1, 0)))
    prev = jnp.moveaxis(prev, 0, 1)
    y_off = jnp.einsum('bclgn,bcgrpn,bclgr->bclgrp', c_c, prev, jnp.exp(cs))
    y = (y_diag + y_off).reshape(bsz, s, SSD_HEADS, P) + xs.reshape(bsz, s, SSD_HEADS, P) * d_skip.astype(jnp.float32)[:, None]
    y = y.reshape(bsz, s, D_INNER) * jax.nn.silu(z.astype(jnp.float32))
    y = _rmsnorm(y.reshape(bsz, s, G, D_INNER // G), jnp.ones((), jnp.float32)).reshape(bsz, s, D_INNER) * norm_w.astype(jnp.float32)
    return y.astype(h.dtype) @ w_out


def _partial_rope(x, cos, sin):
    xf = x.astype(jnp.float32)
    half = ROT_DIM // 2
    x1, x2, rest = xf[..., :half], xf[..., half:ROT_DIM], xf[..., ROT_DIM:]
    c, s = cos[None, :, None, :], sin[None, :, None, :]
    return jnp.concatenate([x1 * c - x2 * s, x2 * c + x1 * s, rest], axis=-1)


def _swa_mixer(h, w_in, q_norm, k_norm, sinks, w_out, cos, sin):
    bsz, s, _ = h.shape
    nb, blk, hd = s // SWA_BLOCK, SWA_BLOCK, SWA_HEAD_DIM
    proj = h @ w_in
    nq, nkv = SWA_Q_HEADS * hd, SWA_KV_HEADS * hd
    q = proj[..., :nq].reshape(bsz, s, SWA_Q_HEADS, hd)
    k = proj[..., nq:nq + nkv].reshape(bsz, s, SWA_KV_HEADS, hd)
    v = proj[..., nq + nkv:].reshape(bsz, s, SWA_KV_HEADS, hd).astype(jnp.float32)
    q = _partial_rope(_rmsnorm(q, q_norm), cos, sin)
    k = _partial_rope(_rmsnorm(k, k_norm), cos, sin)
    qb = q.reshape(bsz, nb, blk, SWA_KV_HEADS, SWA_GQA, hd)
    pad = ((0, 0), (blk, 0), (0, 0), (0, 0))
    kp = jnp.pad(k, pad).reshape(bsz, nb + 1, blk, SWA_KV_HEADS, hd)
    vp = jnp.pad(v, pad).reshape(bsz, nb + 1, blk, SWA_KV_HEADS, hd)
    kw = jnp.concatenate([kp[:, :-1], kp[:, 1:]], axis=2)
    vw = jnp.concatenate([vp[:, :-1], vp[:, 1:]], axis=2)
    scores = jnp.einsum('bnqhgd,bnkhd->bnhgqk', qb, kw) * (hd ** -0.5)
    blk_idx = jnp.arange(nb)[:, None]
    qpos = blk_idx * blk + jnp.arange(blk)[None]
    kpos = blk_idx * blk - blk + jnp.arange(2 * blk)[None]
    diff = qpos[:, :, None] - kpos[:, None, :]
    mask = (diff >= 0) & (diff < WINDOW) & (kpos[:, None, :] >= 0)
    scores = jnp.where(mask[None, :, None, None], scores, -jnp.inf)
    sink = sinks.astype(jnp.float32).reshape(SWA_KV_HEADS, SWA_GQA)[None, None, :, :, None, None]
    m = jnp.maximum(jnp.max(scores, axis=-1, keepdims=True), sink)
    p = jnp.exp(scores - m)
    denom = jnp.sum(p, axis=-1, keepdims=True) + jnp.exp(sink - m)
    o = jnp.einsum('bnhgqk,bnkhd->bnqhgd', p / denom, vw).reshape(bsz, s, nq)
    return o.astype(h.dtype) @ w_out


def _hgrn2_mixer(h, w_in, norm_w, w_out, lb):
    bsz, s, _ = h.shape
    nc, L, H, DK, DV = s // HGRN_CHUNK, HGRN_CHUNK, HGRN_HEADS, HGRN_DK, HGRN_DV
    proj = h @ w_in
    q = proj[..., :HGRN_FD].astype(jnp.float32).reshape(bsz, s, H, DK) * (DK ** -0.5)
    f = lb.astype(jnp.float32) + (1.0 - lb.astype(jnp.float32)) * jax.nn.sigmoid(proj[..., HGRN_FD:2 * HGRN_FD].astype(jnp.float32))
    i = proj[..., 2 * HGRN_FD:2 * HGRN_FD + D_MODEL].astype(jnp.float32).reshape(bsz, s, H, DV)
    g_out = proj[..., 2 * HGRN_FD + D_MODEL:]
    log_f = jnp.log(f).reshape(bsz, s, H, DK)
    kk = (1.0 - f).reshape(bsz, s, H, DK)

    def chunks(t):
        return jnp.moveaxis(t.reshape(bsz, nc, L, H, t.shape[-1]), 1, 0)

    causal = jnp.tril(jnp.ones((L, L), dtype=bool))[None, :, :, None, None]

    def step(state, inp):
        qc, kc, vc, gc = inp
        cum = jnp.cumsum(gc, axis=1)
        decay = jnp.exp(jnp.where(causal, cum[:, :, None] - cum[:, None, :], -jnp.inf))
        attn = jnp.einsum('blhk,blshk,bshk->blsh', qc, decay, kc)
        o = jnp.einsum('blsh,bshv->blhv', attn, vc) + jnp.einsum('blhk,bhkv->blhv', qc * jnp.exp(cum), state)
        total = cum[:, -1]
        k_end = kc * jnp.exp(total[:, None] - cum)
        new_state = jnp.exp(total)[..., None] * state + jnp.einsum('bshk,bshv->bhkv', k_end, vc)
        return new_state, o

    state0 = jnp.zeros((bsz, H, DK, DV), jnp.float32)
    _, o = lax.scan(step, state0, (chunks(q), chunks(kk), chunks(i), chunks(log_f)))
    o = jnp.moveaxis(o, 0, 1).reshape(bsz, s, H, DV)
    o = _rmsnorm(o, norm_w).reshape(bsz, s, D_MODEL) * jax.nn.silu(g_out.astype(jnp.float32))
    return o.astype(h.dtype) @ w_out


def _conv_ffn(h, w_up, conv_w, conv_b, w_down):
    u = _causal_dwconv(h @ w_up, conv_w, conv_b)
    return (jax.nn.silu(u[..., D_FF:]) * u[..., :D_FF]) @ w_down


def setup_inputs(seed: int = 0) -> dict:
    key = jax.random.key(seed)
    ks = iter(jax.random.split(key, 64))
    f32 = jnp.float32
    out_scale = (2.0 * DEPTH) ** -0.5

    def nrm(shape, scale):
        return scale * jax.random.normal(next(ks), shape, f32)

    def gain(shape):
        return 1.0 + nrm(shape, 0.02)

    a_lru = jax.random.uniform(next(ks), (N_LRU, D_RNN), f32, 0.9, 0.999)
    dt0 = jnp.exp(jax.random.uniform(next(ks), (N_SSD, SSD_HEADS), f32, np.log(1e-3), np.log(1e-1)))
    return {
        "x": nrm((BATCH, SEQ, D_MODEL), 1.0),
        "mix_norm": gain((DEPTH, D_MODEL)),
        "ffn_norm": gain((DEPTH, D_MODEL)),
        "ffn_w_up": nrm((DEPTH, D_MODEL, 2 * D_FF), D_MODEL ** -0.5),
        "ffn_conv_w": nrm((DEPTH, FFN_CONV, 2 * D_FF), FFN_CONV ** -0.5),
        "ffn_conv_b": nrm((DEPTH, 2 * D_FF), 0.02),
        "ffn_w_down": nrm((DEPTH, D_FF, D_MODEL), D_FF ** -0.5 * out_scale),
        "lru_w_in": nrm((N_LRU, D_MODEL, 2 * D_RNN), D_MODEL ** -0.5),
        "lru_conv_w": nrm((N_LRU, LRU_CONV, D_RNN), LRU_CONV ** -0.5),
        "lru_conv_b": nrm((N_LRU, D_RNN), 0.02),
        "lru_w_gate": nrm((N_LRU, LRU_BLOCKS, LRU_BLK, 2 * LRU_BLK), LRU_BLK ** -0.5),
        "lru_b_gate": nrm((N_LRU, LRU_BLOCKS, 2 * LRU_BLK), 0.02),
        "lru_lambda": jnp.log(a_lru) - jnp.log1p(-a_lru),
        "lru_w_out": nrm((N_LRU, D_RNN, D_MODEL), D_RNN ** -0.5 * out_scale),
        "ssd_w_in": nrm((N_SSD, D_MODEL, SSD_IN), D_MODEL ** -0.5),
        "ssd_conv_w": nrm((N_SSD, SSD_CONV, SSD_CONV_DIM), SSD_CONV ** -0.5),
        "ssd_conv_b": nrm((N_SSD, SSD_CONV_DIM), 0.02),
        "ssd_dt_bias": dt0 + jnp.log(-jnp.expm1(-dt0)),
        "ssd_a_log": jnp.log(jax.random.uniform(next(ks), (N_SSD, SSD_HEADS), f32, 1.0, 16.0)),
        "ssd_d": gain((N_SSD, SSD_HEADS)),
        "ssd_norm": gain((N_SSD, D_INNER)),
        "ssd_w_out": nrm((N_SSD, D_INNER, D_MODEL), D_INNER ** -0.5 * out_scale),
        "swa_w_in": nrm((N_SWA, D_MODEL, SWA_IN), D_MODEL ** -0.5),
        "swa_q_norm": gain((N_SWA, SWA_HEAD_DIM)),
        "swa_k_norm": gain((N_SWA, SWA_HEAD_DIM)),
        "swa_sinks": nrm((N_SWA, SWA_Q_HEADS), 0.5),
        "swa_w_out": nrm((N_SWA, SWA_Q_HEADS * SWA_HEAD_DIM, D_MODEL), (SWA_Q_HEADS * SWA_HEAD_DIM) ** -0.5 * out_scale),
        "hgrn_w_in": nrm((N_HGRN, D_MODEL, HGRN_IN), D_MODEL ** -0.5),
        "hgrn_norm": gain((N_HGRN, HGRN_DV)),
        "hgrn_w_out": nrm((N_HGRN, D_MODEL, D_MODEL), D_MODEL ** -0.5 * out_scale),
        "hgrn_lower_bounds": nrm((DEPTH, HGRN_FD), 0.5),
    }


def reference(x, mix_norm, ffn_norm, ffn_w_up, ffn_conv_w, ffn_conv_b, ffn_w_down,
              lru_w_in, lru_conv_w, lru_conv_b, lru_w_gate, lru_b_gate, lru_lambda, lru_w_out,
              ssd_w_in, ssd_conv_w, ssd_conv_b, ssd_dt_bias, ssd_a_log, ssd_d, ssd_norm, ssd_w_out,
              swa_w_in, swa_q_norm, swa_k_norm, swa_sinks, swa_w_out,
              hgrn_w_in, hgrn_norm, hgrn_w_out, hgrn_lower_bounds):
    s = x.shape[1]
    inv_freq = ROPE_THETA ** (-jnp.arange(0, ROT_DIM, 2, dtype=jnp.float32) / ROT_DIM)
    ang = jnp.arange(s, dtype=jnp.float32)[:, None] * inv_freq[None]
    cos, sin = jnp.cos(ang), jnp.sin(ang)
    lbs = jnp.cumsum(jax.nn.softmax(hgrn_lower_bounds.astype(jnp.float32), axis=0), axis=0)
    lbs = lbs - lbs[0]
    for i in range(DEPTH):
        kind, j = i % N_MIXERS, i // N_MIXERS
        hn = _rmsnorm(x, mix_norm[i])
        if kind == 0:
            mix = _rg_lru_mixer(hn, lru_w_in[j], lru_conv_w[j], lru_conv_b[j], lru_w_gate[j],
                                lru_b_gate[j], lru_lambda[j], lru_w_out[j])
        elif kind == 1:
            mix = _ssd_mixer(hn, ssd_w_in[j], ssd_conv_w[j], ssd_conv_b[j], ssd_dt_bias[j],
                             ssd_a_log[j], ssd_d[j], ssd_norm[j], ssd_w_out[j])
        elif kind == 2:
            mix = _swa_mixer(hn, swa_w_in[j], swa_q_norm[j], swa_k_norm[j], swa_sinks[j],
                             swa_w_out[j], cos, sin)
        else:
            mix = _hgrn2_mixer(hn, hgrn_w_in[j], hgrn_norm[j], hgrn_w_out[j], lbs[i])
        x = x + mix.astype(x.dtype)
        x = x + _conv_ffn(_rmsnorm(x, ffn_norm[i]), ffn_w_up[i], ffn_conv_w[i], ffn_conv_b[i],
                          ffn_w_down[i]).astype(x.dtype)
    return x
```

```python
import functools

import jax
import jax.numpy as jnp
import numpy as np
from jax import lax
from jax.experimental import pallas as pl
from jax.experimental.pallas import tpu as pltpu

F32 = jnp.float32
BF16 = jnp.bfloat16
EPS = 1e-6

V7X_LANES = 128
V7X_SUBLANES = 8
V7X_VMEM_BYTES = 64 * 1024 * 1024
VMEM_LIMIT_BYTES = V7X_VMEM_BYTES - 8 * 1024 * 1024

D_MODEL = 1024
D_FF = 2816
FFN_CHUNK = 256
N_FFN_CHUNKS = D_FF // FFN_CHUNK

D_RNN = 1280
LRU_BLOCKS = 10
LRU_BLK = 128
LRU_C = 8.0

D_INNER = 2048
SSD_HEADS = 32
SSD_HEAD_DIM = 64
SSD_GROUPS = 4
SSD_STATE = 128
SSD_CHUNK = 128
SSD_GN = SSD_GROUPS * SSD_STATE
SSD_CONV_DIM = D_INNER + 2 * SSD_GN
SSD_MAIN = D_INNER + SSD_CONV_DIM

SWA_HEAD_DIM = 64
SWA_Q_HEADS = 16
SWA_KV_HEADS = 4
SWA_BLOCK = 128
SWA_Q = SWA_Q_HEADS * SWA_HEAD_DIM
SWA_KV = SWA_KV_HEADS * SWA_HEAD_DIM
SWA_IN = SWA_Q + 2 * SWA_KV
ROT_DIM = 16
ROT_HALF = 8
ROPE_THETA = 500000.0

HGRN_HEADS = 8
HGRN_DK = 128
HGRN_CHUNK = 64
HGRN_SUB = 16
HGRN_TILE = 256

CONV_HALO = V7X_SUBLANES


def _cparams(*sem):
    return pltpu.CompilerParams(dimension_semantics=sem, vmem_limit_bytes=VMEM_LIMIT_BYTES)


def _sigmoid(x):
    return 1.0 / (1.0 + jnp.exp(-x))


def _silu(x):
    return x * _sigmoid(x)


def _rms_rows(x, g):
    ms = jnp.mean(x * x, axis=-1, keepdims=True)
    return x * lax.rsqrt(ms + EPS) * g


def _dot(a, b):
    return jnp.dot(a, b, preferred_element_type=F32)


def _dot_nt(a, b):
    return lax.dot_general(a, b, (((1,), (1,)), ((), ())), preferred_element_type=F32)


def _dot_exact_rhs(a, b01):
    a0 = a.astype(BF16)
    r1 = a - a0.astype(F32)
    a1 = r1.astype(BF16)
    a2 = (r1 - a1.astype(F32)).astype(BF16)
    return _dot(a0, b01) + _dot(a1, b01) + _dot(a2, b01)


def _dot_exact_lhs(a01, b):
    b0 = b.astype(BF16)
    r1 = b - b0.astype(F32)
    b1 = r1.astype(BF16)
    b2 = (r1 - b1.astype(F32)).astype(BF16)
    return _dot(a01, b0) + _dot(a01, b1) + _dot(a01, b2)


def _causal_conv(ext, w_ref, bias, taps, rows):
    acc = bias + w_ref[taps - 1:taps, :] * ext[CONV_HALO:, :]
    for k in range(taps - 1):
        shift = taps - 1 - k
        acc = acc + w_ref[k:k + 1, :] * pltpu.roll(ext, shift, 0)[CONV_HALO:, :]
    return acc


def _proj_kernel(x_ref, g_ref, w_ref, o_ref, *, n_chunk):
    hn = _rms_rows(x_ref[...], g_ref[...]).astype(BF16)
    for c in range(0, o_ref.shape[1], n_chunk):
        o_ref[:, c:c + n_chunk] = _dot(hn, w_ref[:, c:c + n_chunk])


def _norm_proj(x2d, gain, w_bf16, name, tm=256):
    t, d = x2d.shape
    n = w_bf16.shape[1]
    n_chunk = min(512, n)
    assert n % n_chunk == 0 and t % tm == 0
    return pl.pallas_call(
        functools.partial(_proj_kernel, n_chunk=n_chunk),
        grid=(t // tm,),
        in_specs=[pl.BlockSpec((tm, d), lambda i: (i, 0)),
                  pl.BlockSpec((1, d), lambda i: (0, 0)),
                  pl.BlockSpec((d, n), lambda i: (0, 0))],
        out_specs=pl.BlockSpec((tm, n), lambda i: (i, 0)),
        out_shape=jax.ShapeDtypeStruct((t, n), F32),
        compiler_params=_cparams("parallel"),
        name=name,
    )(x2d, gain.reshape(1, d), w_bf16)


def _out_kernel(y_ref, w_ref, x_ref, o_ref):
    o_ref[...] = x_ref[...] + _dot(y_ref[...], w_ref[...])


def _out_proj(y_bf16, w_bf16, x2d, name, tm=512):
    t, k = y_bf16.shape
    d = w_bf16.shape[1]
    return pl.pallas_call(
        _out_kernel,
        grid=(t // tm,),
        in_specs=[pl.BlockSpec((tm, k), lambda i: (i, 0)),
                  pl.BlockSpec((k, d), lambda i: (0, 0)),
                  pl.BlockSpec((tm, d), lambda i: (i, 0))],
        out_specs=pl.BlockSpec((tm, d), lambda i: (i, 0)),
        out_shape=jax.ShapeDtypeStruct((t, d), F32),
        compiler_params=_cparams("parallel"),
        name=name,
    )(y_bf16, w_bf16, x2d)


def _lru_core_kernel(p_ref, cw_ref, cb_ref, wg_ref, bg_ref, c_ref, y_ref,
                     halo_ref, h_ref, a_scr, u_scr, *, tm):
    @pl.when(pl.program_id(1) == 0)
    def _():
        halo_ref[...] = jnp.zeros_like(halo_ref)
        h_ref[...] = jnp.zeros_like(h_ref)

    xr = p_ref[:, D_RNN:]
    ext = jnp.concatenate([halo_ref[...], xr], axis=0)
    halo_ref[...] = xr[tm - CONV_HALO:, :]
    xc = _causal_conv(ext, cw_ref, cb_ref[...], 4, tm)

    for n in range(LRU_BLOCKS):
        lo, hi = n * LRU_BLK, (n + 1) * LRU_BLK
        xcn = xc[:, lo:hi]
        g = _dot(xcn.astype(BF16), wg_ref[n]) + bg_ref[:, 2 * lo:2 * hi]
        r = _sigmoid(g[:, :LRU_BLK])
        i = _sigmoid(g[:, LRU_BLK:])
        log_a = c_ref[:, lo:hi] * r
        a = jnp.exp(log_a)
        a_scr[:, lo:hi] = a
        u_scr[:, lo:hi] = jnp.sqrt(-jnp.tanh(log_a) * (a * a + 1.0)) * (i * xcn)

    row = lax.broadcasted_iota(jnp.int32, (V7X_SUBLANES, D_RNN), 0)

    def scan_rows(t, h):
        r0 = pl.multiple_of(t * V7X_SUBLANES, V7X_SUBLANES)
        a = a_scr[pl.ds(r0, V7X_SUBLANES), :]
        u = u_scr[pl.ds(r0, V7X_SUBLANES), :]
        for s in (1, 2, 4):
            keep = row >= s
            a_prev = jnp.where(keep, pltpu.roll(a, s, 0), 1.0)
            u_prev = jnp.where(keep, pltpu.roll(u, s, 0), 0.0)
            u = a * u_prev + u
            a = a * a_prev
        hs = a * h + u
        u_scr[pl.ds(r0, V7X_SUBLANES), :] = hs
        return hs[V7X_SUBLANES - 1:, :]

    h_ref[...] = lax.fori_loop(0, tm // V7X_SUBLANES, scan_rows, h_ref[...])

    gate = p_ref[:, :D_RNN]
    gelu = 0.5 * gate * (1.0 + jnp.tanh(np.sqrt(2.0 / np.pi) * (gate + 0.044715 * (gate * gate * gate))))
    y_ref[...] = (gelu * u_scr[...]).astype(BF16)


def _lru_core(proj, conv_w, conv_b, w_gate_bf16, b_gate, log_a_coef, bsz, seq, tm=256):
    nt = seq // tm
    row_map = lambda b, j: (b * nt + j, 0)
    const2 = lambda b, j: (0, 0)
    return pl.pallas_call(
        functools.partial(_lru_core_kernel, tm=tm),
        grid=(bsz, nt),
        in_specs=[pl.BlockSpec((tm, 2 * D_RNN), row_map),
                  pl.BlockSpec((4, D_RNN), const2),
                  pl.BlockSpec((1, D_RNN), const2),
                  pl.BlockSpec((LRU_BLOCKS, LRU_BLK, 2 * LRU_BLK), lambda b, j: (0, 0, 0)),
                  pl.BlockSpec((1, 2 * D_RNN), const2),
                  pl.BlockSpec((1, D_RNN), const2)],
        out_specs=pl.BlockSpec((tm, D_RNN), row_map),
        out_shape=jax.ShapeDtypeStruct((bsz * seq, D_RNN), BF16),
        scratch_shapes=[pltpu.VMEM((CONV_HALO, D_RNN), F32),
                        pltpu.VMEM((1, D_RNN), F32),
                        pltpu.VMEM((tm, D_RNN), F32),
                        pltpu.VMEM((tm, D_RNN), F32)],
        compiler_params=_cparams("parallel", "arbitrary"),
        name="lru_core",
    )(proj, conv_w, conv_b.reshape(1, D_RNN), w_gate_bf16, b_gate.reshape(1, 2 * D_RNN), log_a_coef)


def _ssd_core_kernel(p_ref, dtr_ref, cw_ref, cb_ref, dtb_ref, a_ref, dskip_ref, nw_ref, tri_ref,
                     y_ref, halo_ref, st_ref):
    L = SSD_CHUNK

    @pl.when(pl.program_id(1) == 0)
    def _():
        halo_ref[...] = jnp.zeros_like(halo_ref)
        st_ref[...] = jnp.zeros_like(st_ref)

    xbc_raw = p_ref[:, D_INNER:]
    ext = jnp.concatenate([halo_ref[...], xbc_raw], axis=0)
    halo_ref[...] = xbc_raw[L - CONV_HALO:, :]
    xbc = _silu(_causal_conv(ext, cw_ref, cb_ref[...], 4, L))

    dt_in = dtr_ref[...] + dtb_ref[...]
    dt = jnp.maximum(dt_in, 0.0) + jnp.log1p(jnp.exp(-jnp.abs(dt_in)))
    cs = _dot_exact_lhs(tri_ref[...], dt * a_ref[...])
    cs_t = cs.T
    dt_t = dt.T

    ri = lax.broadcasted_iota(jnp.int32, (L, L), 0)
    ci = lax.broadcasted_iota(jnp.int32, (L, L), 1)
    causal = ri >= ci
    low_lanes = ci < SSD_HEAD_DIM

    for g in range(SSD_GROUPS):
        bm = xbc[:, D_INNER + g * SSD_STATE:D_INNER + (g + 1) * SSD_STATE]
        cm = xbc[:, D_INNER + SSD_GN + g * SSD_STATE:D_INNER + SSD_GN + (g + 1) * SSD_STATE]
        cb = _dot_nt(cm.astype(BF16), bm.astype(BF16))
        bm_t = bm.T
        y_pairs = []
        for pi in range(SSD_HEADS // SSD_GROUPS // 2):
            pair = g * (SSD_HEADS // SSD_GROUPS // 2) + pi
            lanes = slice(pair * V7X_LANES, (pair + 1) * V7X_LANES)
            xs_pair = xbc[:, lanes]
            rhs = jnp.concatenate([xs_pair.astype(BF16), st_ref[:, lanes].astype(BF16)], axis=0)
            xs_bf = xs_pair.astype(BF16)
            ys, sts, decs = [], [], []
            for h in (2 * pair, 2 * pair + 1):
                col = jnp.broadcast_to(cs[:, h:h + 1], (L, L))
                rowv = cs_t[h:h + 1, :]
                dt_row = dt_t[h:h + 1, :]
                last = cs_t[h:h + 1, L - 1:L]
                decay = jnp.exp(jnp.where(causal, col - rowv, -jnp.inf))
                w_diag = (cb * decay * dt_row).astype(BF16)
                c_off = (cm * jnp.exp(col)).astype(BF16)
                ys.append(_dot(jnp.concatenate([w_diag, c_off], axis=1), rhs))
                w_end = jnp.exp(last - rowv) * dt_row
                sts.append(_dot((bm_t * w_end).astype(BF16), xs_bf))
                decs.append(jnp.exp(last))
            y_pairs.append(jnp.where(low_lanes, ys[0], ys[1]))
            dec = jnp.where(low_lanes[:1, :], decs[0], decs[1])
            st_ref[:, lanes] = st_ref[:, lanes] * dec + jnp.where(low_lanes, sts[0], sts[1])
        glanes = slice(g * 512, (g + 1) * 512)
        y = jnp.concatenate(y_pairs, axis=1) + xbc[:, glanes] * dskip_ref[:, glanes]
        y = y * _silu(p_ref[:, glanes])
        ms = jnp.mean(y * y, axis=-1, keepdims=True)
        y_ref[:, glanes] = (y * lax.rsqrt(ms + EPS) * nw_ref[:, glanes]).astype(BF16)


def _ssd_core(proj, dt_raw, conv_w, conv_b, dt_bias_pad, a_pad, d_skip_exp, norm_w, bsz, seq):
    L = SSD_CHUNK
    nt = seq // L
    row_map = lambda b, j: (b * nt + j, 0)
    const2 = lambda b, j: (0, 0)
    tri = jnp.tril(jnp.ones((L, L), F32)).astype(BF16)
    return pl.pallas_call(
        _ssd_core_kernel,
        grid=(bsz, nt),
        in_specs=[pl.BlockSpec((L, SSD_MAIN), row_map),
                  pl.BlockSpec((L, V7X_LANES), row_map),
                  pl.BlockSpec((4, SSD_CONV_DIM), const2),
                  pl.BlockSpec((1, SSD_CONV_DIM), const2),
                  pl.BlockSpec((1, V7X_LANES), const2),
                  pl.BlockSpec((1, V7X_LANES), const2),
                  pl.BlockSpec((1, D_INNER), const2),
                  pl.BlockSpec((1, D_INNER), const2),
                  pl.BlockSpec((L, L), const2)],
        out_specs=pl.BlockSpec((L, D_INNER), row_map),
        out_shape=jax.ShapeDtypeStruct((bsz * seq, D_INNER), BF16),
        scratch_shapes=[pltpu.VMEM((CONV_HALO, SSD_CONV_DIM), F32),
                        pltpu.VMEM((SSD_STATE, D_INNER), F32)],
        compiler_params=_cparams("parallel", "arbitrary"),
        name="ssd_core",
    )(proj, dt_raw, conv_w, conv_b.reshape(1, SSD_CONV_DIM), dt_bias_pad, a_pad,
      d_skip_exp, norm_w.reshape(1, D_INNER), tri)


def _swa_core_kernel(sink_ref, p_ref, pkv_ref, rope_ref, prope_ref, qn_ref, kn_ref, bd_ref, o_ref):
    blk = SWA_BLOCK
    first_key = jnp.where(pl.program_id(1) == 0, blk, 0)
    bd = bd_ref[...]

    def norm_rope(x, gain, rope):
        ss = _dot_exact_rhs(x * x, bd)
        xn = x * lax.rsqrt(ss * (1.0 / SWA_HEAD_DIM) + EPS) * gain
        return (xn * rope[0] + pltpu.roll(xn, V7X_LANES - ROT_HALF, 1) * rope[1]
                + pltpu.roll(xn, ROT_HALF, 1) * rope[2])

    rope = (rope_ref[0], rope_ref[1], rope_ref[2])
    prope = (prope_ref[0], prope_ref[1], prope_ref[2])

    qi = lax.broadcasted_iota(jnp.int32, (blk, 2 * blk), 0)
    kj = lax.broadcasted_iota(jnp.int32, (blk, 2 * blk), 1)
    window = (kj > qi) & (kj <= qi + blk) & (kj >= first_key)
    lane = lax.broadcasted_iota(jnp.int32, (blk, V7X_LANES), 1)
    low = lane < SWA_HEAD_DIM

    for kp in range(SWA_KV_HEADS // 2):
        klanes = slice(SWA_Q + kp * V7X_LANES, SWA_Q + (kp + 1) * V7X_LANES)
        vlanes = slice(SWA_Q + SWA_KV + kp * V7X_LANES, SWA_Q + SWA_KV + (kp + 1) * V7X_LANES)
        k_cur = norm_rope(p_ref[:, klanes], kn_ref[...], rope)
        k_prev = norm_rope(pkv_ref[:, kp * V7X_LANES:(kp + 1) * V7X_LANES], kn_ref[...], prope)
        kw = jnp.concatenate([k_prev, k_cur], axis=0).astype(BF16)
        vw = jnp.concatenate([pkv_ref[:, SWA_KV + kp * V7X_LANES:SWA_KV + (kp + 1) * V7X_LANES],
                              p_ref[:, vlanes]], axis=0).astype(BF16)
        for pi in range(4):
            pair = kp * 4 + pi
            qlanes = slice(pair * V7X_LANES, (pair + 1) * V7X_LANES)
            q = norm_rope(p_ref[:, qlanes], qn_ref[...], rope) * (SWA_HEAD_DIM ** -0.5)
            outs = []
            for half in range(2):
                qh = jnp.where(low if half == 0 else jnp.logical_not(low), q, 0.0).astype(BF16)
                s = jnp.where(window, _dot_nt(qh, kw), -jnp.inf)
                sink = sink_ref[2 * pair + half]
                m = jnp.maximum(jnp.max(s, axis=-1, keepdims=True), sink)
                p = jnp.exp(s - m)
                denom = jnp.sum(p, axis=-1, keepdims=True) + jnp.exp(sink - m)
                outs.append(_dot(p.astype(BF16), vw) / denom)
            o_ref[:, qlanes] = jnp.where(low, outs[0], outs[1]).astype(BF16)


def _swa_core(sinks_perm, proj, rope_tab, q_gain, k_gain, bsz, seq):
    blk = SWA_BLOCK
    nt = seq // blk
    row_map = lambda b, j: (b * nt + j, 0)
    prev_map = lambda b, j: (jnp.maximum(b * nt + j - 1, 0), SWA_Q // (2 * SWA_KV))
    const2 = lambda b, j: (0, 0)
    half = np.arange(V7X_LANES) // SWA_HEAD_DIM
    bd = jnp.asarray(half[:, None] == half[None, :], BF16)
    return pl.pallas_call(
        _swa_core_kernel,
        grid=(bsz, nt),
        in_specs=[pl.BlockSpec(memory_space=pltpu.SMEM),
                  pl.BlockSpec((blk, SWA_IN), row_map),
                  pl.BlockSpec((blk, 2 * SWA_KV), prev_map),
                  pl.BlockSpec((3, blk, V7X_LANES), lambda b, j: (0, j, 0)),
                  pl.BlockSpec((3, blk, V7X_LANES), lambda b, j: (0, jnp.maximum(j - 1, 0), 0)),
                  pl.BlockSpec((1, V7X_LANES), const2),
                  pl.BlockSpec((1, V7X_LANES), const2),
                  pl.BlockSpec((V7X_LANES, V7X_LANES), const2)],
        out_specs=pl.BlockSpec((blk, SWA_Q), row_map),
        out_shape=jax.ShapeDtypeStruct((bsz * seq, SWA_Q), BF16),
        compiler_params=_cparams("parallel", "parallel"),
        name="swa_core",
    )(sinks_perm, proj, proj, rope_tab, rope_tab, q_gain, k_gain, bd)


def _swa_head_order():
    order = []
    for kp in range(SWA_KV_HEADS // 2):
        for i in range(4):
            order += [4 * (2 * kp) + i, 4 * (2 * kp + 1) + i]
    return np.asarray(order)


def _rope_tables(seq):
    inv_freq = ROPE_THETA ** (-jnp.arange(0, ROT_DIM, 2, dtype=F32) / ROT_DIM)
    ang = jnp.arange(seq, dtype=F32)[:, None] * inv_freq[None]
    cos, sin = jnp.cos(ang), jnp.sin(ang)
    ones = jnp.ones((seq, SWA_HEAD_DIM - ROT_DIM), F32)
    zeros = jnp.zeros((seq, SWA_HEAD_DIM - ROT_DIM), F32)
    z8 = jnp.zeros((seq, ROT_HALF), F32)
    c = jnp.concatenate([cos, cos, ones], axis=1)
    sa = jnp.concatenate([-sin, z8, zeros], axis=1)
    sb = jnp.concatenate([z8, sin, zeros], axis=1)
    tab = jnp.stack([c, sa, sb])
    return jnp.concatenate([tab, tab], axis=2)


def _hgrn_core_kernel(p_ref, lb_ref, nw_ref, tri_ref, y_ref, st_ref, g_scr, k_scr, o_scr, *, tm):
    L, C = HGRN_CHUNK, HGRN_SUB
    fd = HGRN_HEADS * HGRN_DK

    @pl.when(pl.program_id(1) == 0)
    def _():
        st_ref[...] = jnp.zeros_like(st_ref)

    lb = lb_ref[...]
    f = lb + (1.0 - lb) * _sigmoid(p_ref[:, fd:2 * fd])
    g_scr[...] = jnp.log(f)
    k_scr[...] = 1.0 - f

    sub_row = lax.broadcasted_iota(jnp.int32, (C, HGRN_DK), 0)
    tri = tri_ref[...]

    def chunk(c, carry):
        r0 = pl.multiple_of(c * L, L)
        for h in range(HGRN_HEADS):
            lanes = slice(h * HGRN_DK, (h + 1) * HGRN_DK)
            q = p_ref[pl.ds(r0, L), lanes] * (HGRN_DK ** -0.5)
            v = p_ref[pl.ds(r0, L), 2 * fd + h * HGRN_DK:2 * fd + (h + 1) * HGRN_DK]
            kk = k_scr[pl.ds(r0, L), lanes]
            cum = _dot_exact_lhs(tri, g_scr[pl.ds(r0, L), lanes])
            total = cum[L - 1:, :]
            st = st_ref[h]
            v_bf = v.astype(BF16)
            o = _dot_nt((q * jnp.exp(cum)).astype(BF16), st.astype(BF16))
            o_sub = []
            for i in range(L // C):
                qi = q[i * C:(i + 1) * C]
                cumi = cum[i * C:(i + 1) * C]
                oi = o[i * C:(i + 1) * C]
                if i > 0:
                    ref = cum[i * C - 1:i * C]
                    q_t = (qi * jnp.exp(cumi - ref)).astype(BF16)
                    k_t = (kk[:i * C] * jnp.exp(ref - cum[:i * C])).astype(BF16)
                    attn = _dot_nt(q_t, k_t)
                    oi = oi + _dot(attn.astype(BF16), v_bf[:i * C])
                for s in range(C):
                    r = i * C + s
                    e = jnp.exp(jnp.where(sub_row >= s, cumi - cum[r:r + 1], -jnp.inf))
                    w = jnp.sum(qi * kk[r:r + 1] * e, axis=-1, keepdims=True)
                    oi = oi + w * v[r:r + 1]
                o_sub.append(oi)
            o_scr[pl.ds(r0, L), lanes] = jnp.concatenate(o_sub, axis=0)
            k_end = (kk * jnp.exp(total - cum)).astype(BF16)
            st_ref[h] = st * jnp.exp(total) + lax.dot_general(
                v_bf, k_end, (((0,), (0,)), ((), ())), preferred_element_type=F32)
        return carry

    lax.fori_loop(0, tm // L, chunk, 0)

    for h in range(HGRN_HEADS):
        lanes = slice(h * HGRN_DK, (h + 1) * HGRN_DK)
        o = _rms_rows(o_scr[:, lanes], nw_ref[...])
        y_ref[:, lanes] = (o * _silu(p_ref[:, 3 * fd + h * HGRN_DK:3 * fd + (h + 1) * HGRN_DK])).astype(BF16)


def _hgrn_core(proj, lower_bound, norm_w, bsz, seq, tm=HGRN_TILE):
    nt = seq // tm
    fd = HGRN_HEADS * HGRN_DK
    row_map = lambda b, j: (b * nt + j, 0)
    const2 = lambda b, j: (0, 0)
    tri = jnp.tril(jnp.ones((HGRN_CHUNK, HGRN_CHUNK), F32)).astype(BF16)
    return pl.pallas_call(
        functools.partial(_hgrn_core_kernel, tm=tm),
        grid=(bsz, nt),
        in_specs=[pl.BlockSpec((tm, 4 * fd), row_map),
                  pl.BlockSpec((1, fd), const2),
                  pl.BlockSpec((1, HGRN_DK), const2),
                  pl.BlockSpec((HGRN_CHUNK, HGRN_CHUNK), const2)],
        out_specs=pl.BlockSpec((tm, fd), row_map),
        out_shape=jax.ShapeDtypeStruct((bsz * seq, fd), BF16),
        scratch_shapes=[pltpu.VMEM((HGRN_HEADS, HGRN_DK, HGRN_DK), F32),
                        pltpu.VMEM((tm, fd), F32),
                        pltpu.VMEM((tm, fd), F32),
                        pltpu.VMEM((tm, fd), F32)],
        compiler_params=_cparams("parallel", "arbitrary"),
        name="hgrn_core",
    )(proj, lower_bound.reshape(1, fd), norm_w.reshape(1, HGRN_DK), tri)


def _ffn_kernel(x_ref, xh_ref, g_ref, wv_ref, wg_ref, cwv_ref, cwg_ref, cbv_ref, cbg_ref, wd_ref,
                o_ref, hn_scr, acc_scr, *, tm):
    first = pl.program_id(1) == 0
    g = g_ref[...]
    hn_scr[CONV_HALO:, :] = _rms_rows(x_ref[...], g).astype(BF16)
    hn_scr[:CONV_HALO, :] = jnp.where(first, 0.0, _rms_rows(xh_ref[...], g)).astype(BF16)
    acc_scr[...] = x_ref[...]

    def chunk(c, carry):
        hn = hn_scr[...]
        uv = _causal_conv(_dot(hn, wv_ref[c]), cwv_ref.at[c], cbv_ref[c], 3, tm)
        ug = _causal_conv(_dot(hn, wg_ref[c]), cwg_ref.at[c], cbg_ref[c], 3, tm)
        act = (_silu(ug) * uv).astype(BF16)
        acc_scr[...] += _dot(act, wd_ref[c])
        return carry

    lax.fori_loop(0, N_FFN_CHUNKS, chunk, 0)
    o_ref[...] = acc_scr[...]


def _ffn(x2d, gain, w_up, conv_w, conv_b, w_down, bsz, seq, tm=512):
    nt = seq // tm
    hpt = tm // CONV_HALO
    nc, fc = N_FFN_CHUNKS, FFN_CHUNK

    def split(a):
        halves = []
        for part in (a[..., :D_FF], a[..., D_FF:]):
            part = part.reshape(part.shape[:-1] + (nc, fc))
            halves.append(jnp.moveaxis(part, -2, 0))
        return halves

    wv, wg = split(w_up.astype(BF16))
    cwv, cwg = split(conv_w)
    cbv, cbg = split(conv_b.reshape(1, 2 * D_FF))
    wd = w_down.astype(BF16).reshape(nc, fc, D_MODEL)
    row_map = lambda b, j: (b * nt + j, 0)
    halo_map = lambda b, j: (jnp.maximum((b * nt + j) * hpt - 1, 0), 0)
    const3 = lambda b, j: (0, 0, 0)
    return pl.pallas_call(
        functools.partial(_ffn_kernel, tm=tm),
        grid=(bsz, nt),
        in_specs=[pl.BlockSpec((tm, D_MODEL), row_map),
                  pl.BlockSpec((CONV_HALO, D_MODEL), halo_map),
                  pl.BlockSpec((1, D_MODEL), lambda b, j: (0, 0)),
                  pl.BlockSpec((nc, D_MODEL, fc), const3),
                  pl.BlockSpec((nc, D_MODEL, fc), const3),
                  pl.BlockSpec((nc, 3, fc), const3),
                  pl.BlockSpec((nc, 3, fc), const3),
                  pl.BlockSpec((nc, 1, fc), const3),
                  pl.BlockSpec((nc, 1, fc), const3),
                  pl.BlockSpec((nc, fc, D_MODEL), const3)],
        out_specs=pl.BlockSpec((tm, D_MODEL), row_map),
        out_shape=jax.ShapeDtypeStruct((bsz * seq, D_MODEL), F32),
        scratch_shapes=[pltpu.VMEM((CONV_HALO + tm, D_MODEL), BF16),
                        pltpu.VMEM((tm, D_MODEL), F32)],
        compiler_params=_cparams("parallel", "parallel"),
        name="conv_ffn",
    )(x2d, x2d, gain.reshape(1, D_MODEL), wv, wg, cwv, cwg, cbv, cbg, wd)


def _lru_mixer(x2d, gain, w_in, conv_w, conv_b, w_gate, b_gate, lam, w_out, bsz, seq):
    proj = _norm_proj(x2d, gain, w_in.astype(BF16), "lru_in")
    coef = (-LRU_C * jax.nn.softplus(-lam.astype(F32))).reshape(1, D_RNN)
    y = _lru_core(proj, conv_w, conv_b, w_gate.astype(BF16), b_gate, coef, bsz, seq)
    return _out_proj(y, w_out.astype(BF16), x2d, "lru_out")


def _ssd_mixer(x2d, gain, w_in, conv_w, conv_b, dt_bias, a_log, d_skip, norm_w, w_out, bsz, seq):
    pad = V7X_LANES - SSD_HEADS
    proj = _norm_proj(x2d, gain, w_in[:, :SSD_MAIN].astype(BF16), "ssd_in")
    w_dt = jnp.pad(w_in[:, SSD_MAIN:], ((0, 0), (0, pad))).astype(BF16)
    dt_raw = _norm_proj(x2d, gain, w_dt, "ssd_dt")
    dt_bias_pad = jnp.pad(dt_bias.astype(F32), (0, pad)).reshape(1, V7X_LANES)
    a_pad = jnp.pad(-jnp.exp(a_log.astype(F32)), (0, pad)).reshape(1, V7X_LANES)
    d_exp = jnp.repeat(d_skip.astype(F32), SSD_HEAD_DIM).reshape(1, D_INNER)
    y = _ssd_core(proj, dt_raw, conv_w, conv_b, dt_bias_pad, a_pad, d_exp, norm_w, bsz, seq)
    return _out_proj(y, w_out.astype(BF16), x2d, "ssd_out")


def _swa_mixer(x2d, gain, w_in, q_norm, k_norm, sinks, w_out, bsz, seq):
    order = _swa_head_order()
    cols = (order[:, None] * SWA_HEAD_DIM + np.arange(SWA_HEAD_DIM)[None]).reshape(-1)
    w_in_perm = jnp.concatenate([w_in[:, cols], w_in[:, SWA_Q:]], axis=1).astype(BF16)
    proj = _norm_proj(x2d, gain, w_in_perm, "swa_in")
    q_gain = jnp.tile(q_norm.astype(F32), 2).reshape(1, V7X_LANES)
    k_gain = jnp.tile(k_norm.astype(F32), 2).reshape(1, V7X_LANES)
    o = _swa_core(sinks.astype(F32)[order], proj, _rope_tables(seq), q_gain, k_gain, bsz, seq)
    return _out_proj(o, w_out[cols, :].astype(BF16), x2d, "swa_out")


def _hgrn_mixer(x2d, gain, w_in, norm_w, w_out, lower_bound, bsz, seq):
    proj = _norm_proj(x2d, gain, w_in.astype(BF16), "hgrn_in")
    y = _hgrn_core(proj, lower_bound, norm_w, bsz, seq)
    return _out_proj(y, w_out.astype(BF16), x2d, "hgrn_out")


def kernel(x, mix_norm, ffn_norm, ffn_w_up, ffn_conv_w, ffn_conv_b, ffn_w_down, lru_w_in, lru_conv_w, lru_conv_b, lru_w_gate, lru_b_gate, lru_lambda, lru_w_out, ssd_w_in, ssd_conv_w, ssd_conv_b, ssd_dt_bias, ssd_a_log, ssd_d, ssd_norm, ssd_w_out, swa_w_in, swa_q_norm, swa_k_norm, swa_sinks, swa_w_out, hgrn_w_in, hgrn_norm, hgrn_w_out, hgrn_lower_bounds):
    bsz, seq, d = x.shape
    depth = mix_norm.shape[0]
    lbs = jnp.cumsum(jax.nn.softmax(hgrn_lower_bounds.astype(F32), axis=0), axis=0)
    lbs = lbs - lbs[0]
    h = x.reshape(bsz * seq, d)
    for i in range(depth):
        kind, j = i % 4, i // 4
        if kind == 0:
            h = _lru_mixer(h, mix_norm[i], lru_w_in[j], lru_conv_w[j], lru_conv_b[j], lru_w_gate[j],
                           lru_b_gate[j], lru_lambda[j], lru_w_out[j], bsz, seq)
        elif kind == 1:
            h = _ssd_mixer(h, mix_norm[i], ssd_w_in[j], ssd_conv_w[j], ssd_conv_b[j], ssd_dt_bias[j],
                           ssd_a_log[j], ssd_d[j], ssd_norm[j], ssd_w_out[j], bsz, seq)
        elif kind == 2:
            h = _swa_mixer(h, mix_norm[i], swa_w_in[j], swa_q_norm[j], swa_k_norm[j], swa_sinks[j],
                           swa_w_out[j], bsz, seq)
        else:
            h = _hgrn_mixer(h, mix_norm[i], hgrn_w_in[j], hgrn_norm[j], hgrn_w_out[j], lbs[i], bsz, seq)
        h = _ffn(h, ffn_norm[i], ffn_w_up[i], ffn_conv_w[i], ffn_conv_b[i], ffn_w_down[i], bsz, seq)
    return h.reshape(bsz, seq, d)
```

```python
import functools

import jax
import jax.numpy as jnp
import numpy as np
from jax import lax
from jax.experimental import pallas as pl
from jax.experimental.pallas import tpu as pltpu

F32 = jnp.float32
BF16 = jnp.bfloat16
EPS = 1e-6

V7X_LANES = 128
V7X_SUBLANES = 8
V7X_VMEM_BYTES = 64 * 1024 * 1024
VMEM_LIMIT_BYTES = V7X_VMEM_BYTES - 8 * 1024 * 1024

D_MODEL = 1024
D_FF = 2816
FFN_CHUNK = 256
N_FFN_CHUNKS = D_FF // FFN_CHUNK

D_RNN = 1280
LRU_BLOCKS = 10
LRU_BLK = 128
LRU_C = 8.0

D_INNER = 2048
SSD_HEADS = 32
SSD_HEAD_DIM = 64
SSD_GROUPS = 4
SSD_STATE = 128
SSD_CHUNK = 128
SSD_GN = SSD_GROUPS * SSD_STATE
SSD_CONV_DIM = D_INNER + 2 * SSD_GN
SSD_MAIN = D_INNER + SSD_CONV_DIM

SWA_HEAD_DIM = 64
SWA_Q_HEADS = 16
SWA_KV_HEADS = 4
SWA_BLOCK = 128
SWA_Q = SWA_Q_HEADS * SWA_HEAD_DIM
SWA_KV = SWA_KV_HEADS * SWA_HEAD_DIM
SWA_IN = SWA_Q + 2 * SWA_KV
ROT_DIM = 16
ROT_HALF = 8
ROPE_THETA = 500000.0

HGRN_HEADS = 8
HGRN_DK = 128
HGRN_CHUNK = 64
HGRN_SUB = 16
HGRN_TILE = 256

CONV_HALO = V7X_SUBLANES


def _cparams(*sem):
    return pltpu.CompilerParams(dimension_semantics=sem, vmem_limit_bytes=VMEM_LIMIT_BYTES)


def _sigmoid(x):
    return 1.0 / (1.0 + jnp.exp(-x))


def _silu(x):
    return x * _sigmoid(x)


def _rms_rows(x, g):
    ms = jnp.mean(x * x, axis=-1, keepdims=True)
    return x * lax.rsqrt(ms + EPS) * g


def _dot(a, b):
    return jnp.dot(a, b, preferred_element_type=F32)


def _dot_nt(a, b):
    return lax.dot_general(a, b, (((1,), (1,)), ((), ())), preferred_element_type=F32)


def _dot_exact_rhs(a, b01):
    a0 = a.astype(BF16)
    r1 = a - a0.astype(F32)
    a1 = r1.astype(BF16)
    a2 = (r1 - a1.astype(F32)).astype(BF16)
    return _dot(a0, b01) + _dot(a1, b01) + _dot(a2, b01)


def _dot_exact_lhs(a01, b):
    b0 = b.astype(BF16)
    r1 = b - b0.astype(F32)
    b1 = r1.astype(BF16)
    b2 = (r1 - b1.astype(F32)).astype(BF16)
    return _dot(a01, b0) + _dot(a01, b1) + _dot(a01, b2)


def _causal_conv(ext, w_ref, bias, taps, rows):
    acc = bias + w_ref[taps - 1:taps, :] * ext[CONV_HALO:, :]
    for k in range(taps - 1):
        shift = taps - 1 - k
        acc = acc + w_ref[k:k + 1, :] * pltpu.roll(ext, shift, 0)[CONV_HALO:, :]
    return acc


def _proj_kernel(x_ref, g_ref, w_ref, o_ref, *, n_chunk):
    hn = _rms_rows(x_ref[...], g_ref[...]).astype(BF16)
    for c in range(0, o_ref.shape[1], n_chunk):
        o_ref[:, c:c + n_chunk] = _dot(hn, w_ref[:, c:c + n_chunk])


def _norm_proj(x2d, gain, w_bf16, name, tm=256):
    t, d = x2d.shape
    n = w_bf16.shape[1]
    n_chunk = min(512, n)
    assert n % n_chunk == 0 and t % tm == 0
    return pl.pallas_call(
        functools.partial(_proj_kernel, n_chunk=n_chunk),
        grid=(t // tm,),
        in_specs=[pl.BlockSpec((tm, d), lambda i: (i, 0)),
                  pl.BlockSpec((1, d), lambda i: (0, 0)),
                  pl.BlockSpec((d, n), lambda i: (0, 0))],
        out_specs=pl.BlockSpec((tm, n), lambda i: (i, 0)),
        out_shape=jax.ShapeDtypeStruct((t, n), F32),
        compiler_params=_cparams("parallel"),
        name=name,
    )(x2d, gain.reshape(1, d), w_bf16)


def _out_kernel(y_ref, w_ref, x_ref, o_ref):
    o_ref[...] = x_ref[...] + _dot(y_ref[...], w_ref[...])


def _out_proj(y_bf16, w_bf16, x2d, name, tm=512):
    t, k = y_bf16.shape
    d = w_bf16.shape[1]
    return pl.pallas_call(
        _out_kernel,
        grid=(t // tm,),
        in_specs=[pl.BlockSpec((tm, k), lambda i: (i, 0)),
                  pl.BlockSpec((k, d), lambda i: (0, 0)),
                  pl.BlockSpec((tm, d), lambda i: (i, 0))],
        out_specs=pl.BlockSpec((tm, d), lambda i: (i, 0)),
        out_shape=jax.ShapeDtypeStruct((t, d), F32),
        compiler_params=_cparams("parallel"),
        name=name,
    )(y_bf16, w_bf16, x2d)


def _lru_core_kernel(p_ref, cw_ref, cb_ref, wg_ref, bg_ref, c_ref, y_ref,
                     halo_ref, h_ref, a_scr, u_scr, *, tm):
    @pl.when(pl.program_id(1) == 0)
    def _():
        halo_ref[...] = jnp.zeros_like(halo_ref)
        h_ref[...] = jnp.zeros_like(h_ref)

    xr = p_ref[:, D_RNN:]
    ext = jnp.concatenate([halo_ref[...], xr], axis=0)
    halo_ref[...] = xr[tm - CONV_HALO:, :]
    xc = _causal_conv(ext, cw_ref, cb_ref[...], 4, tm)

    for n in range(LRU_BLOCKS):
        lo, hi = n * LRU_BLK, (n + 1) * LRU_BLK
        xcn = xc[:, lo:hi]
        g = _dot(xcn.astype(BF16), wg_ref[n]) + bg_ref[:, 2 * lo:2 * hi]
        r = _sigmoid(g[:, :LRU_BLK])
        i = _sigmoid(g[:, LRU_BLK:])
        log_a = c_ref[:, lo:hi] * r
        a = jnp.exp(log_a)
        a_scr[:, lo:hi] = a
        u_scr[:, lo:hi] = jnp.sqrt(-jnp.tanh(log_a) * (a * a + 1.0)) * (i * xcn)

    row = lax.broadcasted_iota(jnp.int32, (V7X_SUBLANES, D_RNN), 0)

    def scan_rows(t, h):
        r0 = pl.multiple_of(t * V7X_SUBLANES, V7X_SUBLANES)
        a = a_scr[pl.ds(r0, V7X_SUBLANES), :]
        u = u_scr[pl.ds(r0, V7X_SUBLANES), :]
        for s in (1, 2, 4):
            keep = row >= s
            a_prev = jnp.where(keep, pltpu.roll(a, s, 0), 1.0)
            u_prev = jnp.where(keep, pltpu.roll(u, s, 0), 0.0)
            u = a * u_prev + u
            a = a * a_prev
        hs = a * h + u
        u_scr[pl.ds(r0, V7X_SUBLANES), :] = hs
        return hs[V7X_SUBLANES - 1:, :]

    h_ref[...] = lax.fori_loop(0, tm // V7X_SUBLANES, scan_rows, h_ref[...])

    gate = p_ref[:, :D_RNN]
    gelu = 0.5 * gate * (1.0 + jnp.tanh(np.sqrt(2.0 / np.pi) * (gate + 0.044715 * (gate * gate * gate))))
    y_ref[...] = (gelu * u_scr[...]).astype(BF16)


def _lru_core(proj, conv_w, conv_b, w_gate_bf16, b_gate, log_a_coef, bsz, seq, tm=256):
    nt = seq // tm
    row_map = lambda b, j: (b * nt + j, 0)
    const2 = lambda b, j: (0, 0)
    return pl.pallas_call(
        functools.partial(_lru_core_kernel, tm=tm),
        grid=(bsz, nt),
        in_specs=[pl.BlockSpec((tm, 2 * D_RNN), row_map),
                  pl.BlockSpec((4, D_RNN), const2),
                  pl.BlockSpec((1, D_RNN), const2),
                  pl.BlockSpec((LRU_BLOCKS, LRU_BLK, 2 * LRU_BLK), lambda b, j: (0, 0, 0)),
                  pl.BlockSpec((1, 2 * D_RNN), const2),
                  pl.BlockSpec((1, D_RNN), const2)],
        out_specs=pl.BlockSpec((tm, D_RNN), row_map),
        out_shape=jax.ShapeDtypeStruct((bsz * seq, D_RNN), BF16),
        scratch_shapes=[pltpu.VMEM((CONV_HALO, D_RNN), F32),
                        pltpu.VMEM((1, D_RNN), F32),
                        pltpu.VMEM((tm, D_RNN), F32),
                        pltpu.VMEM((tm, D_RNN), F32)],
        compiler_params=_cparams("parallel", "arbitrary"),
        name="lru_core",
    )(proj, conv_w, conv_b.reshape(1, D_RNN), w_gate_bf16, b_gate.reshape(1, 2 * D_RNN), log_a_coef)


def _ssd_core_kernel(p_ref, dtr_ref, cw_ref, cb_ref, dtb_ref, a_ref, dskip_ref, nw_ref, tri_ref,
                     y_ref, halo_ref, st_ref):
    L = SSD_CHUNK

    @pl.when(pl.program_id(1) == 0)
    def _():
        halo_ref[...] = jnp.zeros_like(halo_ref)
        st_ref[...] = jnp.zeros_like(st_ref)

    xbc_raw = p_ref[:, D_INNER:]
    ext = jnp.concatenate([halo_ref[...], xbc_raw], axis=0)
    halo_ref[...] = xbc_raw[L - CONV_HALO:, :]
    xbc = _silu(_causal_conv(ext, cw_ref, cb_ref[...], 4, L))

    dt_in = dtr_ref[...] + dtb_ref[...]
    dt = jnp.maximum(dt_in, 0.0) + jnp.log1p(jnp.exp(-jnp.abs(dt_in)))
    cs = _dot_exact_lhs(tri_ref[...], dt * a_ref[...])
    cs_t = cs.T
    dt_t = dt.T

    ri = lax.broadcasted_iota(jnp.int32, (L, L), 0)
    ci = lax.broadcasted_iota(jnp.int32, (L, L), 1)
    causal = ri >= ci
    low_lanes = ci < SSD_HEAD_DIM

    for g in range(SSD_GROUPS):
        bm = xbc[:, D_INNER + g * SSD_STATE:D_INNER + (g + 1) * SSD_STATE]
        cm = xbc[:, D_INNER + SSD_GN + g * SSD_STATE:D_INNER + SSD_GN + (g + 1) * SSD_STATE]
        cb = _dot_nt(cm.astype(BF16), bm.astype(BF16))
        bm_t = bm.T
        y_pairs = []
        for pi in range(SSD_HEADS // SSD_GROUPS // 2):
            pair = g * (SSD_HEADS // SSD_GROUPS // 2) + pi
            lanes = slice(pair * V7X_LANES, (pair + 1) * V7X_LANES)
            xs_pair = xbc[:, lanes]
            rhs = jnp.concatenate([xs_pair.astype(BF16), st_ref[:, lanes].astype(BF16)], axis=0)
            xs_bf = xs_pair.astype(BF16)
            ys, sts, decs = [], [], []
            for h in (2 * pair, 2 * pair + 1):
                col = jnp.broadcast_to(cs[:, h:h + 1], (L, L))
                rowv = cs_t[h:h + 1, :]
                dt_row = dt_t[h:h + 1, :]
                last = cs_t[h:h + 1, L - 1:L]
                decay = jnp.exp(jnp.where(causal, col - rowv, -jnp.inf))
                w_diag = (cb * decay * dt_row).astype(BF16)
                c_off = (cm * jnp.exp(col)).astype(BF16)
                ys.append(_dot(jnp.concatenate([w_diag, c_off], axis=1), rhs))
                w_end = jnp.exp(last - rowv) * dt_row
                sts.append(_dot((bm_t * w_end).astype(BF16), xs_bf))
                decs.append(jnp.exp(last))
            y_pairs.append(jnp.where(low_lanes, ys[0], ys[1]))
            dec = jnp.where(low_lanes[:1, :], decs[0], decs[1])
            st_ref[:, lanes] = st_ref[:, lanes] * dec + jnp.where(low_lanes, sts[0], sts[1])
        glanes = slice(g * 512, (g + 1) * 512)
        y = jnp.concatenate(y_pairs, axis=1) + xbc[:, glanes] * dskip_ref[:, glanes]
        y = y * _silu(p_ref[:, glanes])
        ms = jnp.mean(y * y, axis=-1, keepdims=True)
        y_ref[:, glanes] = (y * lax.rsqrt(ms + EPS) * nw_ref[:, glanes]).astype(BF16)


def _ssd_core(proj, dt_raw, conv_w, conv_b, dt_bias_pad, a_pad, d_skip_exp, norm_w, bsz, seq):
    L = SSD_CHUNK
    nt = seq // L
    row_map = lambda b, j: (b * nt + j, 0)
    const2 = lambda b, j: (0, 0)
    tri = jnp.tril(jnp.ones((L, L), F32)).astype(BF16)
    return pl.pallas_call(
        _ssd_core_kernel,
        grid=(bsz, nt),
        in_specs=[pl.BlockSpec((L, SSD_MAIN), row_map),
                  pl.BlockSpec((L, V7X_LANES), row_map),
                  pl.BlockSpec((4, SSD_CONV_DIM), const2),
                  pl.BlockSpec((1, SSD_CONV_DIM), const2),
                  pl.BlockSpec((1, V7X_LANES), const2),
                  pl.BlockSpec((1, V7X_LANES), const2),
                  pl.BlockSpec((1, D_INNER), const2),
                  pl.BlockSpec((1, D_INNER), const2),
                  pl.BlockSpec((L, L), const2)],
        out_specs=pl.BlockSpec((L, D_INNER), row_map),
        out_shape=jax.ShapeDtypeStruct((bsz * seq, D_INNER), BF16),
        scratch_shapes=[pltpu.VMEM((CONV_HALO, SSD_CONV_DIM), F32),
                        pltpu.VMEM((SSD_STATE, D_INNER), F32)],
        compiler_params=_cparams("parallel", "arbitrary"),
        name="ssd_core",
    )(proj, dt_raw, conv_w, conv_b.reshape(1, SSD_CONV_DIM), dt_bias_pad, a_pad,
      d_skip_exp, norm_w.reshape(1, D_INNER), tri)


def _swa_core_kernel(sink_ref, p_ref, pkv_ref, rope_ref, prope_ref, qn_ref, kn_ref, bd_ref, o_ref):
    blk = SWA_BLOCK
    first_key = jnp.where(pl.program_id(1) == 0, blk, 0)
    bd = bd_ref[...]

    def norm_rope(x, gain, rope):
        ss = _dot_exact_rhs(x * x, bd)
        xn = x * lax.rsqrt(ss * (1.0 / SWA_HEAD_DIM) + EPS) * gain
        return (xn * rope[0] + pltpu.roll(xn, V7X_LANES - ROT_HALF, 1) * rope[1]
                + pltpu.roll(xn, ROT_HALF, 1) * rope[2])

    rope = (rope_ref[0], rope_ref[1], rope_ref[2])
    prope = (prope_ref[0], prope_ref[1], prope_ref[2])

    qi = lax.broadcasted_iota(jnp.int32, (blk, 2 * blk), 0)
    kj = lax.broadcasted_iota(jnp.int32, (blk, 2 * blk), 1)
    window = (kj > qi) & (kj <= qi + blk) & (kj >= first_key)
    lane = lax.broadcasted_iota(jnp.int32, (blk, V7X_LANES), 1)
    low = lane < SWA_HEAD_DIM

    for kp in range(SWA_KV_HEADS // 2):
        klanes = slice(SWA_Q + kp * V7X_LANES, SWA_Q + (kp + 1) * V7X_LANES)
        vlanes = slice(SWA_Q + SWA_KV + kp * V7X_LANES, SWA_Q + SWA_KV + (kp + 1) * V7X_LANES)
        k_cur = norm_rope(p_ref[:, klanes], kn_ref[...], rope)
        k_prev = norm_rope(pkv_ref[:, kp * V7X_LANES:(kp + 1) * V7X_LANES], kn_ref[...], prope)
        kw = jnp.concatenate([k_prev, k_cur], axis=0).astype(BF16)
        vw = jnp.concatenate([pkv_ref[:, SWA_KV + kp * V7X_LANES:SWA_KV + (kp + 1) * V7X_LANES],
                              p_ref[:, vlanes]], axis=0).astype(BF16)
        for pi in range(4):
            pair = kp * 4 + pi
            qlanes = slice(pair * V7X_LANES, (pair + 1) * V7X_LANES)
            q = norm_rope(p_ref[:, qlanes], qn_ref[...], rope) * (SWA_HEAD_DIM ** -0.5)
            outs = []
            for half in range(2):
                qh = jnp.where(low if half == 0 else jnp.logical_not(low), q, 0.0).astype(BF16)
                s = jnp.where(window, _dot_nt(qh, kw), -jnp.inf)
                sink = sink_ref[2 * pair + half]
                m = jnp.maximum(jnp.max(s, axis=-1, keepdims=True), sink)
                p = jnp.exp(s - m)
                denom = jnp.sum(p, axis=-1, keepdims=True) + jnp.exp(sink - m)
                outs.append(_dot(p.astype(BF16), vw) / denom)
            o_ref[:, qlanes] = jnp.where(low, outs[0], outs[1]).astype(BF16)


def _swa_core(sinks_perm, proj, rope_tab, q_gain, k_gain, bsz, seq):
    blk = SWA_BLOCK
    nt = seq // blk
    row_map = lambda b, j: (b * nt + j, 0)
    prev_map = lambda b, j: (jnp.maximum(b * nt + j - 1, 0), SWA_Q // (2 * SWA_KV))
    const2 = lambda b, j: (0, 0)
    half = np.arange(V7X_LANES) // SWA_HEAD_DIM
    bd = jnp.asarray(half[:, None] == half[None, :], BF16)
    return pl.pallas_call(
        _swa_core_kernel,
        grid=(bsz, nt),
        in_specs=[pl.BlockSpec(memory_space=pltpu.SMEM),
                  pl.BlockSpec((blk, SWA_IN), row_map),
                  pl.BlockSpec((blk, 2 * SWA_KV), prev_map),
                  pl.BlockSpec((3, blk, V7X_LANES), lambda b, j: (0, j, 0)),
                  pl.BlockSpec((3, blk, V7X_LANES), lambda b, j: (0, jnp.maximum(j - 1, 0), 0)),
                  pl.BlockSpec((1, V7X_LANES), const2),
                  pl.BlockSpec((1, V7X_LANES), const2),
                  pl.BlockSpec((V7X_LANES, V7X_LANES), const2)],
        out_specs=pl.BlockSpec((blk, SWA_Q), row_map),
        out_shape=jax.ShapeDtypeStruct((bsz * seq, SWA_Q), BF16),
        compiler_params=_cparams("parallel", "parallel"),
        name="swa_core",
    )(sinks_perm, proj, proj, rope_tab, rope_tab, q_gain, k_gain, bd)


def _swa_head_order():
    order = []
    for kp in range(SWA_KV_HEADS // 2):
        for i in range(4):
            order += [4 * (2 * kp) + i, 4 * (2 * kp + 1) + i]
    return np.asarray(order)


def _rope_tables(seq):
    inv_freq = ROPE_THETA ** (-jnp.arange(0, ROT_DIM, 2, dtype=F32) / ROT_DIM)
    ang = jnp.arange(seq, dtype=F32)[:, None] * inv_freq[None]
    cos, sin = jnp.cos(ang), jnp.sin(ang)
    ones = jnp.ones((seq, SWA_HEAD_DIM - ROT_DIM), F32)
    zeros = jnp.zeros((seq, SWA_HEAD_DIM - ROT_DIM), F32)
    z8 = jnp.zeros((seq, ROT_HALF), F32)
    c = jnp.concatenate([cos, cos, ones], axis=1)
    sa = jnp.concatenate([-sin, z8, zeros], axis=1)
    sb = jnp.concatenate([z8, sin, zeros], axis=1)
    tab = jnp.stack([c, sa, sb])
    return jnp.concatenate([tab, tab], axis=2)


def _hgrn_chunk_exact(h, r0, q, v, kk, cum, st_ref, o_scr, sub_row):
    L, C = HGRN_CHUNK, HGRN_SUB
    lanes = slice(h * HGRN_DK, (h + 1) * HGRN_DK)
    total = cum[L - 1:, :]
    st = st_ref[h]
    v_bf = v.astype(BF16)
    o = _dot_nt((q * jnp.exp(cum)).astype(BF16), st.astype(BF16))
    o_sub = []
    for i in range(L // C):
        qi = q[i * C:(i + 1) * C]
        cumi = cum[i * C:(i + 1) * C]
        oi = o[i * C:(i + 1) * C]
        if i > 0:
            ref = cum[i * C - 1:i * C]
            q_t = (qi * jnp.exp(cumi - ref)).astype(BF16)
            k_t = (kk[:i * C] * jnp.exp(ref - cum[:i * C])).astype(BF16)
            attn = _dot_nt(q_t, k_t)
            oi = oi + _dot(attn.astype(BF16), v_bf[:i * C])
        for s in range(C):
            r = i * C + s
            e = jnp.exp(jnp.where(sub_row >= s, cumi - cum[r:r + 1], -jnp.inf))
            w = jnp.sum(qi * kk[r:r + 1] * e, axis=-1, keepdims=True)
            oi = oi + w * v[r:r + 1]
        o_sub.append(oi)
    o_scr[pl.ds(r0, L), lanes] = jnp.concatenate(o_sub, axis=0)
    k_end = (kk * jnp.exp(total - cum)).astype(BF16)
    st_ref[h] = st * jnp.exp(total) + lax.dot_general(
        v_bf, k_end, (((0,), (0,)), ((), ())), preferred_element_type=F32)


def _hgrn_chunk_bounded(r0, qs, vs, kks, cums, st_ref, o_scr, causal):
    L = HGRN_CHUNK
    heads = range(HGRN_HEADS)
    mids = [c[L // 2 - 1:L // 2, :] for c in cums]
    totals = [c[L - 1:, :] for c in cums]
    sts = [st_ref[h] for h in heads]
    v_bf = [v.astype(BF16) for v in vs]
    qe = [qs[h] * jnp.exp(cums[h] - mids[h]) for h in heads]
    ke = [kks[h] * jnp.exp(mids[h] - cums[h]) for h in heads]
    attn = [jnp.where(causal, _dot_nt(qe[h].astype(BF16), ke[h].astype(BF16)), 0.0).astype(BF16) for h in heads]
    o_inter = [_dot_nt((qe[h] * jnp.exp(mids[h])).astype(BF16), sts[h].astype(BF16)) for h in heads]
    o_intra = [_dot(attn[h], v_bf[h]) for h in heads]
    upd = [lax.dot_general(v_bf[h], (ke[h] * jnp.exp(totals[h] - mids[h])).astype(BF16),
                           (((0,), (0,)), ((), ())), preferred_element_type=F32) for h in heads]
    for h in heads:
        o_scr[pl.ds(r0, L), h * HGRN_DK:(h + 1) * HGRN_DK] = o_intra[h] + o_inter[h]
        st_ref[h] = sts[h] * jnp.exp(totals[h]) + upd[h]


def _hgrn_core_kernel(bounded_ref, p_ref, lb_ref, nw_ref, tri_ref, y_ref, st_ref, g_scr, k_scr, o_scr, *, tm):
    L, C = HGRN_CHUNK, HGRN_SUB
    fd = HGRN_HEADS * HGRN_DK

    @pl.when(pl.program_id(1) == 0)
    def _():
        st_ref[...] = jnp.zeros_like(st_ref)

    lb = lb_ref[...]
    f = lb + (1.0 - lb) * _sigmoid(p_ref[:, fd:2 * fd])
    g_scr[...] = jnp.log(f)
    k_scr[...] = 1.0 - f

    sub_row = lax.broadcasted_iota(jnp.int32, (C, HGRN_DK), 0)
    causal = lax.broadcasted_iota(jnp.int32, (L, L), 0) >= lax.broadcasted_iota(jnp.int32, (L, L), 1)
    tri = tri_ref[...]

    def operands(h, r0):
        lanes = slice(h * HGRN_DK, (h + 1) * HGRN_DK)
        q = p_ref[pl.ds(r0, L), lanes] * (HGRN_DK ** -0.5)
        v = p_ref[pl.ds(r0, L), 2 * fd + h * HGRN_DK:2 * fd + (h + 1) * HGRN_DK]
        kk = k_scr[pl.ds(r0, L), lanes]
        cum = _dot_exact_lhs(tri, g_scr[pl.ds(r0, L), lanes])
        return q, v, kk, cum

    def chunk(c, carry):
        r0 = pl.multiple_of(c * L, L)

        @pl.when(bounded_ref[0] == 1)
        def _():
            cum_all = _dot_exact_lhs(tri, g_scr[pl.ds(r0, L), :])
            qs, vs, kks, cums = [], [], [], []
            for h in range(HGRN_HEADS):
                lanes = slice(h * HGRN_DK, (h + 1) * HGRN_DK)
                qs.append(p_ref[pl.ds(r0, L), lanes] * (HGRN_DK ** -0.5))
                vs.append(p_ref[pl.ds(r0, L), 2 * fd + h * HGRN_DK:2 * fd + (h + 1) * HGRN_DK])
                kks.append(k_scr[pl.ds(r0, L), lanes])
                cums.append(cum_all[:, lanes])
            _hgrn_chunk_bounded(r0, qs, vs, kks, cums, st_ref, o_scr, causal)

        @pl.when(bounded_ref[0] == 0)
        def _():
            for h in range(HGRN_HEADS):
                _hgrn_chunk_exact(h, r0, *operands(h, r0), st_ref, o_scr, sub_row)

        return carry

    lax.fori_loop(0, tm // L, chunk, 0)

    for h in range(HGRN_HEADS):
        lanes = slice(h * HGRN_DK, (h + 1) * HGRN_DK)
        o = _rms_rows(o_scr[:, lanes], nw_ref[...])
        y_ref[:, lanes] = (o * _silu(p_ref[:, 3 * fd + h * HGRN_DK:3 * fd + (h + 1) * HGRN_DK])).astype(BF16)


HGRN_MAX_EXPONENT = 80.0


def _hgrn_core(proj, lower_bound, norm_w, bsz, seq, tm=HGRN_TILE):
    nt = seq // tm
    fd = HGRN_HEADS * HGRN_DK
    row_map = lambda b, j: (b * nt + j, 0)
    const2 = lambda b, j: (0, 0)
    tri = jnp.tril(jnp.ones((HGRN_CHUNK, HGRN_CHUNK), F32)).astype(BF16)
    worst = -(HGRN_CHUNK // 2) * jnp.log(jnp.min(lower_bound))
    bounded = (worst <= HGRN_MAX_EXPONENT).astype(jnp.int32).reshape(1)
    return pl.pallas_call(
        functools.partial(_hgrn_core_kernel, tm=tm),
        grid=(bsz, nt),
        in_specs=[pl.BlockSpec(memory_space=pltpu.SMEM),
                  pl.BlockSpec((tm, 4 * fd), row_map),
                  pl.BlockSpec((1, fd), const2),
                  pl.BlockSpec((1, HGRN_DK), const2),
                  pl.BlockSpec((HGRN_CHUNK, HGRN_CHUNK), const2)],
        out_specs=pl.BlockSpec((tm, fd), row_map),
        out_shape=jax.ShapeDtypeStruct((bsz * seq, fd), BF16),
        scratch_shapes=[pltpu.VMEM((HGRN_HEADS, HGRN_DK, HGRN_DK), F32),
                        pltpu.VMEM((tm, fd), F32),
                        pltpu.VMEM((tm, fd), F32),
                        pltpu.VMEM((tm, fd), F32)],
        compiler_params=_cparams("parallel", "arbitrary"),
        name="hgrn_core",
    )(bounded, proj, lower_bound.reshape(1, fd), norm_w.reshape(1, HGRN_DK), tri)


def _ffn_kernel(x_ref, xh_ref, g_ref, wu_ref, cw_ref, cb_ref, wd_ref, o_ref, hn_scr, u_scr, act_scr, *, tm):
    fc = FFN_CHUNK
    first = pl.program_id(1) == 0
    g = g_ref[...]
    hn_scr[CONV_HALO:, :] = _rms_rows(x_ref[...], g).astype(BF16)
    hn_scr[:CONV_HALO, :] = jnp.where(first, 0.0, _rms_rows(xh_ref[...], g)).astype(BF16)

    def up(c):
        for half in range(2):
            cols = slice(half * D_FF + c * fc, half * D_FF + (c + 1) * fc)
            u_scr[c % 2, half] = _dot(hn_scr[...], wu_ref[:, cols])

    def gate(c):
        u = []
        for half in range(2):
            cols = slice(half * D_FF + c * fc, half * D_FF + (c + 1) * fc)
            u.append(_causal_conv(u_scr[c % 2, half], cw_ref.at[:, cols], cb_ref[:, cols], 3, tm))
        act_scr[:, c * fc:(c + 1) * fc] = (_silu(u[1]) * u[0]).astype(BF16)

    up(0)
    for c in range(N_FFN_CHUNKS):
        if c + 1 < N_FFN_CHUNKS:
            up(c + 1)
        gate(c)
    o_ref[...] = x_ref[...] + _dot(act_scr[...], wd_ref[...])


def _ffn(x2d, gain, w_up, conv_w, conv_b, w_down, bsz, seq, tm=512):
    nt = seq // tm
    hpt = tm // CONV_HALO
    fc = FFN_CHUNK
    wu = w_up.astype(BF16)
    wd = w_down.astype(BF16)
    row_map = lambda b, j: (b * nt + j, 0)
    halo_map = lambda b, j: (jnp.maximum((b * nt + j) * hpt - 1, 0), 0)
    const2 = lambda b, j: (0, 0)
    return pl.pallas_call(
        functools.partial(_ffn_kernel, tm=tm),
        grid=(bsz, nt),
        in_specs=[pl.BlockSpec((tm, D_MODEL), row_map),
                  pl.BlockSpec((CONV_HALO, D_MODEL), halo_map),
                  pl.BlockSpec((1, D_MODEL), const2),
                  pl.BlockSpec((D_MODEL, 2 * D_FF), const2),
                  pl.BlockSpec((3, 2 * D_FF), const2),
                  pl.BlockSpec((1, 2 * D_FF), const2),
                  pl.BlockSpec((D_FF, D_MODEL), const2)],
        out_specs=pl.BlockSpec((tm, D_MODEL), row_map),
        out_shape=jax.ShapeDtypeStruct((bsz * seq, D_MODEL), F32),
        scratch_shapes=[pltpu.VMEM((CONV_HALO + tm, D_MODEL), BF16),
                        pltpu.VMEM((2, 2, CONV_HALO + tm, fc), F32),
                        pltpu.VMEM((tm, D_FF), BF16)],
        compiler_params=_cparams("parallel", "parallel"),
        name="conv_ffn",
    )(x2d, x2d, gain.reshape(1, D_MODEL), wu, conv_w, conv_b.reshape(1, 2 * D_FF), wd)


def _lru_mixer(x2d, gain, w_in, conv_w, conv_b, w_gate, b_gate, lam, w_out, bsz, seq):
    proj = _norm_proj(x2d, gain, w_in.astype(BF16), "lru_in")
    coef = (-LRU_C * jax.nn.softplus(-lam.astype(F32))).reshape(1, D_RNN)
    y = _lru_core(proj, conv_w, conv_b, w_gate.astype(BF16), b_gate, coef, bsz, seq)
    return _out_proj(y, w_out.astype(BF16), x2d, "lru_out")


def _ssd_mixer(x2d, gain, w_in, conv_w, conv_b, dt_bias, a_log, d_skip, norm_w, w_out, bsz, seq):
    pad = V7X_LANES - SSD_HEADS
    proj = _norm_proj(x2d, gain, w_in[:, :SSD_MAIN].astype(BF16), "ssd_in")
    w_dt = jnp.pad(w_in[:, SSD_MAIN:], ((0, 0), (0, pad))).astype(BF16)
    dt_raw = _norm_proj(x2d, gain, w_dt, "ssd_dt")
    dt_bias_pad = jnp.pad(dt_bias.astype(F32), (0, pad)).reshape(1, V7X_LANES)
    a_pad = jnp.pad(-jnp.exp(a_log.astype(F32)), (0, pad)).reshape(1, V7X_LANES)
    d_exp = jnp.repeat(d_skip.astype(F32), SSD_HEAD_DIM).reshape(1, D_INNER)
    y = _ssd_core(proj, dt_raw, conv_w, conv_b, dt_bias_pad, a_pad, d_exp, norm_w, bsz, seq)
    return _out_proj(y, w_out.astype(BF16), x2d, "ssd_out")


def _swa_mixer(x2d, gain, w_in, q_norm, k_norm, sinks, w_out, bsz, seq):
    order = _swa_head_order()
    cols = (order[:, None] * SWA_HEAD_DIM + np.arange(SWA_HEAD_DIM)[None]).reshape(-1)
    w_in_perm = jnp.concatenate([w_in[:, cols], w_in[:, SWA_Q:]], axis=1).astype(BF16)
    proj = _norm_proj(x2d, gain, w_in_perm, "swa_in")
    q_gain = jnp.tile(q_norm.astype(F32), 2).reshape(1, V7X_LANES)
    k_gain = jnp.tile(k_norm.astype(F32), 2).reshape(1, V7X_LANES)
    o = _swa_core(sinks.astype(F32)[order], proj, _rope_tables(seq), q_gain, k_gain, bsz, seq)
    return _out_proj(o, w_out[cols, :].astype(BF16), x2d, "swa_out")


def _hgrn_mixer(x2d, gain, w_in, norm_w, w_out, lower_bound, bsz, seq):
    proj = _norm_proj(x2d, gain, w_in.astype(BF16), "hgrn_in")
    y = _hgrn_core(proj, lower_bound, norm_w, bsz, seq)
    return _out_proj(y, w_out.astype(BF16), x2d, "hgrn_out")


def kernel(x, mix_norm, ffn_norm, ffn_w_up, ffn_conv_w, ffn_conv_b, ffn_w_down, lru_w_in, lru_conv_w, lru_conv_b, lru_w_gate, lru_b_gate, lru_lambda, lru_w_out, ssd_w_in, ssd_conv_w, ssd_conv_b, ssd_dt_bias, ssd_a_log, ssd_d, ssd_norm, ssd_w_out, swa_w_in, swa_q_norm, swa_k_norm, swa_sinks, swa_w_out, hgrn_w_in, hgrn_norm, hgrn_w_out, hgrn_lower_bounds):
    bsz, seq, d = x.shape
    depth = mix_norm.shape[0]
    lbs = jnp.cumsum(jax.nn.softmax(hgrn_lower_bounds.astype(F32), axis=0), axis=0)
    lbs = lbs - lbs[0]
    h = x.reshape(bsz * seq, d)
    for i in range(depth):
        kind, j = i % 4, i // 4
        if kind == 0:
            h = _lru_mixer(h, mix_norm[i], lru_w_in[j], lru_conv_w[j], lru_conv_b[j], lru_w_gate[j],
                           lru_b_gate[j], lru_lambda[j], lru_w_out[j], bsz, seq)
        elif kind == 1:
            h = _ssd_mixer(h, mix_norm[i], ssd_w_in[j], ssd_conv_w[j], ssd_conv_b[j], ssd_dt_bias[j],
                           ssd_a_log[j], ssd_d[j], ssd_norm[j], ssd_w_out[j], bsz, seq)
        elif kind == 2:
            h = _swa_mixer(h, mix_norm[i], swa_w_in[j], swa_q_norm[j], swa_k_norm[j], swa_sinks[j],
                           swa_w_out[j], bsz, seq)
        else:
            h = _hgrn_mixer(h, mix_norm[i], hgrn_w_in[j], hgrn_norm[j], hgrn_w_out[j], lbs[i], bsz, seq)
        h = _ffn(h, ffn_norm[i], ffn_w_up[i], ffn_conv_w[i], ffn_conv_b[i], ffn_w_down[i], bsz, seq)
    return h.reshape(bsz, seq, d)
```

```python
import functools

import jax
import jax.numpy as jnp
import numpy as np
from jax import lax
from jax.experimental import pallas as pl
from jax.experimental.pallas import tpu as pltpu

F32 = jnp.float32
BF16 = jnp.bfloat16
EPS = 1e-6

V7X_LANES = 128
V7X_SUBLANES = 8
V7X_VMEM_BYTES = 64 * 1024 * 1024
VMEM_LIMIT_BYTES = V7X_VMEM_BYTES - 8 * 1024 * 1024

D_MODEL = 1024
D_FF = 2816
FFN_CHUNK = 256
N_FFN_CHUNKS = D_FF // FFN_CHUNK

D_RNN = 1280
LRU_BLOCKS = 10
LRU_BLK = 128
LRU_C = 8.0

D_INNER = 2048
SSD_HEADS = 32
SSD_HEAD_DIM = 64
SSD_GROUPS = 4
SSD_STATE = 128
SSD_CHUNK = 128
SSD_GN = SSD_GROUPS * SSD_STATE
SSD_CONV_DIM = D_INNER + 2 * SSD_GN
SSD_MAIN = D_INNER + SSD_CONV_DIM

SWA_HEAD_DIM = 64
SWA_Q_HEADS = 16
SWA_KV_HEADS = 4
SWA_BLOCK = 128
SWA_Q = SWA_Q_HEADS * SWA_HEAD_DIM
SWA_KV = SWA_KV_HEADS * SWA_HEAD_DIM
SWA_IN = SWA_Q + 2 * SWA_KV
ROT_DIM = 16
ROT_HALF = 8
ROPE_THETA = 500000.0

HGRN_HEADS = 8
HGRN_DK = 128
HGRN_CHUNK = 64
HGRN_SUB = 16
HGRN_TILE = 256

CONV_HALO = V7X_SUBLANES


def _cparams(*sem):
    return pltpu.CompilerParams(dimension_semantics=sem, vmem_limit_bytes=VMEM_LIMIT_BYTES)


def _sigmoid(x):
    return 1.0 / (1.0 + jnp.exp(-x))


def _silu(x):
    return x * _sigmoid(x)


def _rms_rows(x, g):
    ms = jnp.mean(x * x, axis=-1, keepdims=True)
    return x * lax.rsqrt(ms + EPS) * g


def _dot(a, b):
    return jnp.dot(a, b, preferred_element_type=F32)


def _dot_nt(a, b):
    return lax.dot_general(a, b, (((1,), (1,)), ((), ())), preferred_element_type=F32)


def _dot_exact_rhs(a, b01):
    a0 = a.astype(BF16)
    r1 = a - a0.astype(F32)
    a1 = r1.astype(BF16)
    a2 = (r1 - a1.astype(F32)).astype(BF16)
    return _dot(a0, b01) + _dot(a1, b01) + _dot(a2, b01)


def _dot_exact_lhs(a01, b):
    b0 = b.astype(BF16)
    r1 = b - b0.astype(F32)
    b1 = r1.astype(BF16)
    b2 = (r1 - b1.astype(F32)).astype(BF16)
    return _dot(a01, b0) + _dot(a01, b1) + _dot(a01, b2)


def _causal_conv(ext, w_ref, bias, taps, rows):
    acc = bias + w_ref[taps - 1:taps, :] * ext[CONV_HALO:, :]
    for k in range(taps - 1):
        shift = taps - 1 - k
        acc = acc + w_ref[k:k + 1, :] * pltpu.roll(ext, shift, 0)[CONV_HALO:, :]
    return acc


def _proj_kernel(*refs, tm, halo, width, n_param, epilogue, tail):
    if halo:
        x_ref, xh_ref, g_ref, w_ref = refs[:4]
        rest = refs[4:]
    else:
        x_ref, g_ref, w_ref = refs[:3]
        rest = refs[3:]
    params, outs = rest[:n_param], rest[n_param:-2]
    hn_scr, u_scr = rest[-2:]
    g = g_ref[...]
    hn_scr[halo:, :] = _rms_rows(x_ref[...], g).astype(BF16)
    if halo:
        first = pl.program_id(1) == 0
        hn_scr[:halo, :] = jnp.where(first, 0.0, _rms_rows(xh_ref[...], g)).astype(BF16)
    nc = w_ref.shape[1] // width

    def project(c):
        u_scr[c % 2] = _dot(hn_scr[...], w_ref[:, c * width:(c + 1) * width])

    project(0)
    for c in range(nc):
        if c + 1 < nc:
            project(c + 1)
        epilogue(c, u_scr.at[c % 2], params, outs)
    if tail is not None:
        tail(hn_scr, params, outs)


def _norm_proj(x2d, gain, w_bf16, params, out_widths, out_dtypes, epilogue, name, bsz, seq,
               tm=256, width=512, halo=0, tail=None):
    d = x2d.shape[1]
    nt = seq // tm
    row_map = lambda b, j: (b * nt + j, 0)
    whole = lambda a: pl.BlockSpec(a.shape, lambda b, j: (0,) * a.ndim)
    in_specs = [pl.BlockSpec((tm, d), row_map)]
    args = [x2d]
    if halo:
        hpt = tm // halo
        in_specs.append(pl.BlockSpec((halo, d), lambda b, j: (jnp.maximum((b * nt + j) * hpt - 1, 0), 0)))
        args.append(x2d)
    gain2 = gain.reshape(1, d)
    in_specs += [whole(gain2), whole(w_bf16)] + [whole(p) for p in params]
    args += [gain2, w_bf16] + list(params)
    return pl.pallas_call(
        functools.partial(_proj_kernel, tm=tm, halo=halo, width=width, n_param=len(params),
                          epilogue=epilogue, tail=tail),
        grid=(bsz, nt),
        in_specs=in_specs,
        out_specs=[pl.BlockSpec((tm, w), row_map) for w in out_widths],
        out_shape=[jax.ShapeDtypeStruct((bsz * seq, w), dt) for w, dt in zip(out_widths, out_dtypes)],
        scratch_shapes=[pltpu.VMEM((halo + tm, d), BF16),
                        pltpu.VMEM((2, halo + tm, width), F32)],
        compiler_params=_cparams("parallel", "parallel"),
        name=name,
    )(*args)


def _out_kernel(y_ref, w_ref, x_ref, o_ref):
    o_ref[...] = x_ref[...] + _dot(y_ref[...], w_ref[...])


def _out_proj(y_bf16, w_bf16, x2d, name, tm=512):
    t, k = y_bf16.shape
    d = w_bf16.shape[1]
    return pl.pallas_call(
        _out_kernel,
        grid=(t // tm,),
        in_specs=[pl.BlockSpec((tm, k), lambda i: (i, 0)),
                  pl.BlockSpec((k, d), lambda i: (0, 0)),
                  pl.BlockSpec((tm, d), lambda i: (i, 0))],
        out_specs=pl.BlockSpec((tm, d), lambda i: (i, 0)),
        out_shape=jax.ShapeDtypeStruct((t, d), F32),
        compiler_params=_cparams("parallel"),
        name=name,
    )(y_bf16, w_bf16, x2d)


LRU_WIDTH = 256


def _lru_epilogue(c, u_ref, params, outs):
    cw_ref, cb_ref, wg_ref, bg_ref, coef_ref = params
    gelu_ref, a_ref, b_ref = outs
    half = D_RNN // LRU_WIDTH
    if c < half:
        gate = u_ref[CONV_HALO:, :]
        gelu_ref[:, c * LRU_WIDTH:(c + 1) * LRU_WIDTH] = 0.5 * gate * (
            1.0 + jnp.tanh(np.sqrt(2.0 / np.pi) * (gate + 0.044715 * (gate * gate * gate))))
        return
    cols = slice((c - half) * LRU_WIDTH, (c - half + 1) * LRU_WIDTH)
    xc = _causal_conv(u_ref[...], cw_ref.at[:, cols], cb_ref[:, cols], 4, None)
    for k in range(LRU_WIDTH // LRU_BLK):
        n = (c - half) * (LRU_WIDTH // LRU_BLK) + k
        lo, hi = n * LRU_BLK, (n + 1) * LRU_BLK
        xcn = xc[:, k * LRU_BLK:(k + 1) * LRU_BLK]
        g = _dot(xcn.astype(BF16), wg_ref[n]) + bg_ref[:, 2 * lo:2 * hi]
        r = _sigmoid(g[:, :LRU_BLK])
        i = _sigmoid(g[:, LRU_BLK:])
        log_a = coef_ref[:, lo:hi] * r
        a = jnp.exp(log_a)
        a_ref[:, lo:hi] = a
        b_ref[:, lo:hi] = jnp.sqrt(-jnp.tanh(log_a) * (a * a + 1.0)) * (i * xcn)


def _lru_scan_kernel(gelu_ref, a_ref, b_ref, w_ref, x_ref, o_ref, h_ref, hs_scr, *, tm):
    @pl.when(pl.program_id(1) == 0)
    def _():
        h_ref[...] = jnp.zeros_like(h_ref)

    row = lax.broadcasted_iota(jnp.int32, (V7X_SUBLANES, D_RNN), 0)

    def scan_rows(t, h):
        r0 = pl.multiple_of(t * V7X_SUBLANES, V7X_SUBLANES)
        a = a_ref[pl.ds(r0, V7X_SUBLANES), :]
        u = b_ref[pl.ds(r0, V7X_SUBLANES), :]
        for s in (1, 2, 4):
            keep = row >= s
            a_prev = jnp.where(keep, pltpu.roll(a, s, 0), 1.0)
            u_prev = jnp.where(keep, pltpu.roll(u, s, 0), 0.0)
            u = a * u_prev + u
            a = a * a_prev
        hs = a * h + u
        hs_scr[pl.ds(r0, V7X_SUBLANES), :] = hs
        return hs[V7X_SUBLANES - 1:, :]

    h_ref[...] = lax.fori_loop(0, tm // V7X_SUBLANES, scan_rows, h_ref[...], unroll=2)
    y = (gelu_ref[...] * hs_scr[...]).astype(BF16)
    o_ref[...] = x_ref[...] + _dot(y, w_ref[...])


def _lru_scan(gelu, a, b, w_out_bf16, x2d, bsz, seq, tm=256):
    nt = seq // tm
    row_map = lambda bb, j: (bb * nt + j, 0)
    spec = pl.BlockSpec((tm, D_RNN), row_map)
    xspec = pl.BlockSpec((tm, D_MODEL), row_map)
    return pl.pallas_call(
        functools.partial(_lru_scan_kernel, tm=tm),
        grid=(bsz, nt),
        in_specs=[spec, spec, spec, pl.BlockSpec((D_RNN, D_MODEL), lambda bb, j: (0, 0)), xspec],
        out_specs=xspec,
        out_shape=jax.ShapeDtypeStruct((bsz * seq, D_MODEL), F32),
        scratch_shapes=[pltpu.VMEM((1, D_RNN), F32),
                        pltpu.VMEM((tm, D_RNN), F32)],
        compiler_params=_cparams("parallel", "arbitrary"),
        name="lru_scan",
    )(gelu, a, b, w_out_bf16, x2d)


SSD_WIDTH = 512


def _softplus(x):
    return jnp.maximum(x, 0.0) + jnp.log1p(jnp.exp(-jnp.abs(x)))


def _ssd_epilogue(c, u_ref, params, outs):
    cw_ref, cb_ref = params[:2]
    zs_ref, xbc_ref = outs[:2]
    nz = D_INNER // SSD_WIDTH
    if c < nz:
        zs_ref[:, c * SSD_WIDTH:(c + 1) * SSD_WIDTH] = _silu(u_ref[CONV_HALO:, :])
    else:
        cols = slice((c - nz) * SSD_WIDTH, (c - nz + 1) * SSD_WIDTH)
        xbc_ref[:, cols] = _silu(_causal_conv(u_ref[...], cw_ref.at[:, cols], cb_ref[:, cols], 4, None))


def _ssd_tail(hn_scr, params, outs):
    wdt_ref, dtb_ref = params[2:]
    outs[2][...] = _softplus(_dot(hn_scr[CONV_HALO:, :], wdt_ref[...]) + dtb_ref[...])


def _ssd_core_kernel(zs_ref, xbc_ref, dt_ref, a_ref, dskip_ref, nw_ref, tri_ref, y_ref, st_ref):
    L = SSD_CHUNK
    hpg = SSD_HEADS // SSD_GROUPS

    @pl.when(pl.program_id(1) == 0)
    def _():
        st_ref[...] = jnp.zeros_like(st_ref)

    dt = dt_ref[...]
    cs = _dot_exact_lhs(tri_ref[...], dt * a_ref[...])
    cs_t = cs.T
    dt_t = dt.T

    ri = lax.broadcasted_iota(jnp.int32, (L, L), 0)
    ci = lax.broadcasted_iota(jnp.int32, (L, L), 1)
    causal = ri >= ci
    low_lanes = ci < SSD_HEAD_DIM

    bms = [xbc_ref[:, D_INNER + g * SSD_STATE:D_INNER + (g + 1) * SSD_STATE] for g in range(SSD_GROUPS)]
    cms = [xbc_ref[:, D_INNER + SSD_GN + g * SSD_STATE:D_INNER + SSD_GN + (g + 1) * SSD_STATE]
           for g in range(SSD_GROUPS)]
    cbs = [_dot_nt(cms[g].astype(BF16), bms[g].astype(BF16)) for g in range(SSD_GROUPS)]
    bm_ts = [bm.T for bm in bms]

    for g in range(SSD_GROUPS):
        heads = range(g * hpg, (g + 1) * hpg)
        lhs_y, lhs_s, decs = {}, {}, {}
        for h in heads:
            col = jnp.broadcast_to(cs[:, h:h + 1], (L, L))
            rowv = cs_t[h:h + 1, :]
            dt_row = dt_t[h:h + 1, :]
            last = cs_t[h:h + 1, L - 1:L]
            decay = jnp.exp(jnp.where(causal, col - rowv, -jnp.inf))
            w_diag = (cbs[g] * decay * dt_row).astype(BF16)
            c_off = (cms[g] * jnp.exp(col)).astype(BF16)
            lhs_y[h] = jnp.concatenate([w_diag, c_off], axis=1)
            lhs_s[h] = (bm_ts[g] * (jnp.exp(last - rowv) * dt_row)).astype(BF16)
            decs[h] = jnp.exp(last)
        rhs, xs_bf = {}, {}
        for pair in range(g * hpg // 2, (g + 1) * hpg // 2):
            lanes = slice(pair * V7X_LANES, (pair + 1) * V7X_LANES)
            xs_bf[pair] = xbc_ref[:, lanes].astype(BF16)
            rhs[pair] = jnp.concatenate([xs_bf[pair], st_ref[:, lanes].astype(BF16)], axis=0)
        ys = {h: _dot(lhs_y[h], rhs[h // 2]) for h in heads}
        sts = {h: _dot(lhs_s[h], xs_bf[h // 2]) for h in heads}
        y_pairs = []
        for pair in range(g * hpg // 2, (g + 1) * hpg // 2):
            lanes = slice(pair * V7X_LANES, (pair + 1) * V7X_LANES)
            h0, h1 = 2 * pair, 2 * pair + 1
            y_pairs.append(jnp.where(low_lanes, ys[h0], ys[h1]))
            dec = jnp.where(low_lanes[:1, :], decs[h0], decs[h1])
            st_ref[:, lanes] = st_ref[:, lanes] * dec + jnp.where(low_lanes, sts[h0], sts[h1])
        glanes = slice(g * 512, (g + 1) * 512)
        y = jnp.concatenate(y_pairs, axis=1) + xbc_ref[:, glanes] * dskip_ref[:, glanes]
        y = y * zs_ref[:, glanes]
        ms = jnp.mean(y * y, axis=-1, keepdims=True)
        y_ref[:, glanes] = (y * lax.rsqrt(ms + EPS) * nw_ref[:, glanes]).astype(BF16)


def _ssd_core(zs, xbc, dt, a_pad, d_skip_exp, norm_w, bsz, seq):
    L = SSD_CHUNK
    nt = seq // L
    row_map = lambda b, j: (b * nt + j, 0)
    const2 = lambda b, j: (0, 0)
    tri = jnp.tril(jnp.ones((L, L), F32)).astype(BF16)
    return pl.pallas_call(
        _ssd_core_kernel,
        grid=(bsz, nt),
        in_specs=[pl.BlockSpec((L, D_INNER), row_map),
                  pl.BlockSpec((L, SSD_CONV_DIM), row_map),
                  pl.BlockSpec((L, V7X_LANES), row_map),
                  pl.BlockSpec((1, V7X_LANES), const2),
                  pl.BlockSpec((1, D_INNER), const2),
                  pl.BlockSpec((1, D_INNER), const2),
                  pl.BlockSpec((L, L), const2)],
        out_specs=pl.BlockSpec((L, D_INNER), row_map),
        out_shape=jax.ShapeDtypeStruct((bsz * seq, D_INNER), BF16),
        scratch_shapes=[pltpu.VMEM((SSD_STATE, D_INNER), F32)],
        compiler_params=_cparams("parallel", "arbitrary"),
        name="ssd_core",
    )(zs, xbc, dt, a_pad, d_skip_exp, norm_w.reshape(1, D_INNER), tri)


SWA_WIDTH = 512


def _swa_epilogue(c, u_ref, params, outs):
    rope_ref, qn_ref, kn_ref, bd_ref = params
    out_ref = outs[0]
    tm = u_ref.shape[0]
    r0 = pl.multiple_of(pl.program_id(1) * tm, tm)
    rope = [rope_ref[k, pl.ds(r0, tm), :] for k in range(3)]
    for t in range(SWA_WIDTH // V7X_LANES):
        col = c * SWA_WIDTH + t * V7X_LANES
        x = u_ref[:, t * V7X_LANES:(t + 1) * V7X_LANES]
        if col < SWA_Q + SWA_KV:
            is_q = col < SWA_Q
            sq = x * x
            s0 = sq.astype(BF16)
            s1 = (sq - s0.astype(F32)).astype(BF16)
            ss = _dot(s0, bd_ref[...]) + _dot(s1, bd_ref[...])
            xn = x * lax.rsqrt(ss * (1.0 / SWA_HEAD_DIM) + EPS) * (qn_ref[...] if is_q else kn_ref[...])
            x = (xn * rope[0] + pltpu.roll(xn, V7X_LANES - ROT_HALF, 1) * rope[1]
                 + pltpu.roll(xn, ROT_HALF, 1) * rope[2])
            if is_q:
                x = x * (SWA_HEAD_DIM ** -0.5)
        out_ref[:, col:col + V7X_LANES] = x.astype(BF16)


def _swa_core_kernel(sink_ref, p_ref, pkv_ref, o_ref):
    blk = SWA_BLOCK
    first_key = jnp.where(pl.program_id(1) == 0, blk, 0)
    qi = lax.broadcasted_iota(jnp.int32, (blk, 2 * blk), 0)
    kj = lax.broadcasted_iota(jnp.int32, (blk, 2 * blk), 1)
    window = (kj > qi) & (kj <= qi + blk) & (kj >= first_key)
    lane = lax.broadcasted_iota(jnp.int32, (blk, V7X_LANES), 1)
    low = lane < SWA_HEAD_DIM
    n_pairs = SWA_Q // V7X_LANES

    kws, vws = [], []
    for kp in range(SWA_KV_HEADS // 2):
        kws.append(jnp.concatenate([pkv_ref[:, kp * V7X_LANES:(kp + 1) * V7X_LANES],
                                    p_ref[:, SWA_Q + kp * V7X_LANES:SWA_Q + (kp + 1) * V7X_LANES]], axis=0))
        vws.append(jnp.concatenate([pkv_ref[:, SWA_KV + kp * V7X_LANES:SWA_KV + (kp + 1) * V7X_LANES],
                                    p_ref[:, SWA_Q + SWA_KV + kp * V7X_LANES:
                                          SWA_Q + SWA_KV + (kp + 1) * V7X_LANES]], axis=0))
    half_masks = [jnp.where(lane[:1, :] < SWA_HEAD_DIM, 1.0, 0.0).astype(BF16),
                  jnp.where(lane[:1, :] < SWA_HEAD_DIM, 0.0, 1.0).astype(BF16)]
    kws = [[kw * hm for hm in half_masks] for kw in kws]
    scores = []
    for pair in range(n_pairs):
        q = p_ref[:, pair * V7X_LANES:(pair + 1) * V7X_LANES]
        for half in range(2):
            scores.append(jnp.where(window, _dot_nt(q, kws[pair // 4][half]), -jnp.inf))
    probs, denoms = [], []
    for i, s in enumerate(scores):
        sink = sink_ref[i]
        m = jnp.maximum(jnp.max(s, axis=-1, keepdims=True), sink)
        p = jnp.exp(s - m)
        denoms.append(jnp.sum(p, axis=-1, keepdims=True) + jnp.exp(sink - m))
        probs.append(p.astype(BF16))
    outs = [_dot(probs[i], vws[i // 8]) / denoms[i] for i in range(2 * n_pairs)]
    for pair in range(n_pairs):
        o_ref[:, pair * V7X_LANES:(pair + 1) * V7X_LANES] = jnp.where(
            low, outs[2 * pair], outs[2 * pair + 1]).astype(BF16)


def _swa_core(sinks_perm, qkv, bsz, seq):
    blk = SWA_BLOCK
    nt = seq // blk
    row_map = lambda b, j: (b * nt + j, 0)
    prev_map = lambda b, j: (jnp.maximum(b * nt + j - 1, 0), SWA_Q // (2 * SWA_KV))
    return pl.pallas_call(
        _swa_core_kernel,
        grid=(bsz, nt),
        in_specs=[pl.BlockSpec(memory_space=pltpu.SMEM),
                  pl.BlockSpec((blk, SWA_IN), row_map),
                  pl.BlockSpec((blk, 2 * SWA_KV), prev_map)],
        out_specs=pl.BlockSpec((blk, SWA_Q), row_map),
        out_shape=jax.ShapeDtypeStruct((bsz * seq, SWA_Q), BF16),
        compiler_params=_cparams("parallel", "parallel"),
        name="swa_core",
    )(sinks_perm, qkv, qkv)


def _swa_head_order():
    order = []
    for kp in range(SWA_KV_HEADS // 2):
        for i in range(4):
            order += [4 * (2 * kp) + i, 4 * (2 * kp + 1) + i]
    return np.asarray(order)


def _rope_tables(seq):
    inv_freq = ROPE_THETA ** (-jnp.arange(0, ROT_DIM, 2, dtype=F32) / ROT_DIM)
    ang = jnp.arange(seq, dtype=F32)[:, None] * inv_freq[None]
    cos, sin = jnp.cos(ang), jnp.sin(ang)
    ones = jnp.ones((seq, SWA_HEAD_DIM - ROT_DIM), F32)
    zeros = jnp.zeros((seq, SWA_HEAD_DIM - ROT_DIM), F32)
    z8 = jnp.zeros((seq, ROT_HALF), F32)
    c = jnp.concatenate([cos, cos, ones], axis=1)
    sa = jnp.concatenate([-sin, z8, zeros], axis=1)
    sb = jnp.concatenate([z8, sin, zeros], axis=1)
    tab = jnp.stack([c, sa, sb])
    return jnp.concatenate([tab, tab], axis=2)


HGRN_WIDTH = 512
HGRN_FD = HGRN_HEADS * HGRN_DK
HGRN_COLS = 5 * HGRN_FD


def _hgrn_epilogue(c, u_ref, params, outs):
    lb_ref = params[0]
    out_ref = outs[0]
    kind, k = divmod(c, HGRN_FD // HGRN_WIDTH)
    cols = slice(k * HGRN_WIDTH, (k + 1) * HGRN_WIDTH)
    u = u_ref[...]
    if kind == 0:
        out_ref[:, cols] = u * (HGRN_DK ** -0.5)
    elif kind == 1:
        lb = lb_ref[:, cols]
        f = lb + (1.0 - lb) * _sigmoid(u)
        out_ref[:, HGRN_FD + k * HGRN_WIDTH:HGRN_FD + (k + 1) * HGRN_WIDTH] = jnp.log(f)
        out_ref[:, 2 * HGRN_FD + k * HGRN_WIDTH:2 * HGRN_FD + (k + 1) * HGRN_WIDTH] = 1.0 - f
    elif kind == 2:
        out_ref[:, 3 * HGRN_FD + k * HGRN_WIDTH:3 * HGRN_FD + (k + 1) * HGRN_WIDTH] = u
    else:
        out_ref[:, 4 * HGRN_FD + k * HGRN_WIDTH:4 * HGRN_FD + (k + 1) * HGRN_WIDTH] = _silu(u)


def _hgrn_chunk_exact(h, r0, q, v, kk, cum, st_ref, o_scr, sub_row):
    L, C = HGRN_CHUNK, HGRN_SUB
    lanes = slice(h * HGRN_DK, (h + 1) * HGRN_DK)
    total = cum[L - 1:, :]
    st = st_ref[h]
    v_bf = v.astype(BF16)
    o = _dot_nt((q * jnp.exp(cum)).astype(BF16), st.astype(BF16))
    o_sub = []
    for i in range(L // C):
        qi = q[i * C:(i + 1) * C]
        cumi = cum[i * C:(i + 1) * C]
        oi = o[i * C:(i + 1) * C]
        if i > 0:
            ref = cum[i * C - 1:i * C]
            q_t = (qi * jnp.exp(cumi - ref)).astype(BF16)
            k_t = (kk[:i * C] * jnp.exp(ref - cum[:i * C])).astype(BF16)
            attn = _dot_nt(q_t, k_t)
            oi = oi + _dot(attn.astype(BF16), v_bf[:i * C])
        for s in range(C):
            r = i * C + s
            e = jnp.exp(jnp.where(sub_row >= s, cumi - cum[r:r + 1], -jnp.inf))
            w = jnp.sum(qi * kk[r:r + 1] * e, axis=-1, keepdims=True)
            oi = oi + w * v[r:r + 1]
        o_sub.append(oi)
    o_scr[pl.ds(r0, L), lanes] = jnp.concatenate(o_sub, axis=0)
    k_end = (kk * jnp.exp(total - cum)).astype(BF16)
    st_ref[h] = st * jnp.exp(total) + lax.dot_general(
        v_bf, k_end, (((0,), (0,)), ((), ())), preferred_element_type=F32)


def _hgrn_chunk_bounded(r0, qs, vs, kks, cums, st_ref, o_scr, causal):
    L = HGRN_CHUNK
    heads = range(HGRN_HEADS)
    mids = [c[L // 2 - 1:L // 2, :] for c in cums]
    totals = [c[L - 1:, :] for c in cums]
    sts = [st_ref[h] for h in heads]
    v_bf = [v.astype(BF16) for v in vs]
    qe = [qs[h] * jnp.exp(cums[h] - mids[h]) for h in heads]
    ke = [kks[h] * jnp.exp(mids[h] - cums[h]) for h in heads]
    attn = [jnp.where(causal, _dot_nt(qe[h].astype(BF16), ke[h].astype(BF16)), 0.0).astype(BF16) for h in heads]
    o_inter = [_dot_nt((qe[h] * jnp.exp(mids[h])).astype(BF16), sts[h].astype(BF16)) for h in heads]
    o_intra = [_dot(attn[h], v_bf[h]) for h in heads]
    upd = [lax.dot_general(v_bf[h], (ke[h] * jnp.exp(totals[h] - mids[h])).astype(BF16),
                           (((0,), (0,)), ((), ())), preferred_element_type=F32) for h in heads]
    for h in heads:
        o_scr[pl.ds(r0, L), h * HGRN_DK:(h + 1) * HGRN_DK] = o_intra[h] + o_inter[h]
        st_ref[h] = sts[h] * jnp.exp(totals[h]) + upd[h]


def _hgrn_core_kernel(bounded_ref, p_ref, nw_ref, tri_ref, w_ref, x_ref, out_ref, st_ref, o_scr, y_scr, *, tm):
    L, C = HGRN_CHUNK, HGRN_SUB
    fd = HGRN_FD

    @pl.when(pl.program_id(1) == 0)
    def _():
        st_ref[...] = jnp.zeros_like(st_ref)

    sub_row = lax.broadcasted_iota(jnp.int32, (C, HGRN_DK), 0)
    causal = lax.broadcasted_iota(jnp.int32, (L, L), 0) >= lax.broadcasted_iota(jnp.int32, (L, L), 1)
    tri = tri_ref[...]

    def operands(h, r0):
        lanes = slice(h * HGRN_DK, (h + 1) * HGRN_DK)
        q = p_ref[pl.ds(r0, L), lanes]
        v = p_ref[pl.ds(r0, L), 3 * fd + h * HGRN_DK:3 * fd + (h + 1) * HGRN_DK]
        kk = p_ref[pl.ds(r0, L), 2 * fd + h * HGRN_DK:2 * fd + (h + 1) * HGRN_DK]
        cum = _dot_exact_lhs(tri, p_ref[pl.ds(r0, L), fd + h * HGRN_DK:fd + (h + 1) * HGRN_DK])
        return q, v, kk, cum

    def chunk(c, carry):
        r0 = pl.multiple_of(c * L, L)

        @pl.when(bounded_ref[0] == 1)
        def _():
            cum_all = _dot_exact_lhs(tri, p_ref[pl.ds(r0, L), fd:2 * fd])
            qs, vs, kks, cums = [], [], [], []
            for h in range(HGRN_HEADS):
                lanes = slice(h * HGRN_DK, (h + 1) * HGRN_DK)
                qs.append(p_ref[pl.ds(r0, L), lanes])
                vs.append(p_ref[pl.ds(r0, L), 3 * fd + h * HGRN_DK:3 * fd + (h + 1) * HGRN_DK])
                kks.append(p_ref[pl.ds(r0, L), 2 * fd + h * HGRN_DK:2 * fd + (h + 1) * HGRN_DK])
                cums.append(cum_all[:, lanes])
            _hgrn_chunk_bounded(r0, qs, vs, kks, cums, st_ref, o_scr, causal)

        @pl.when(bounded_ref[0] == 0)
        def _():
            for h in range(HGRN_HEADS):
                _hgrn_chunk_exact(h, r0, *operands(h, r0), st_ref, o_scr, sub_row)

        return carry

    lax.fori_loop(0, tm // L, chunk, 0)

    for h in range(HGRN_HEADS):
        lanes = slice(h * HGRN_DK, (h + 1) * HGRN_DK)
        o = _rms_rows(o_scr[:, lanes], nw_ref[...])
        y_scr[:, lanes] = (o * p_ref[:, 4 * fd + h * HGRN_DK:4 * fd + (h + 1) * HGRN_DK]).astype(BF16)
    out_ref[...] = x_ref[...] + _dot(y_scr[...], w_ref[...])


HGRN_MAX_EXPONENT = 80.0


def _hgrn_core(prep, lower_bound, norm_w, w_out_bf16, x2d, bsz, seq, tm=HGRN_TILE):
    nt = seq // tm
    fd = HGRN_FD
    row_map = lambda b, j: (b * nt + j, 0)
    const2 = lambda b, j: (0, 0)
    tri = jnp.tril(jnp.ones((HGRN_CHUNK, HGRN_CHUNK), F32)).astype(BF16)
    worst = -(HGRN_CHUNK // 2) * jnp.log(jnp.min(lower_bound))
    bounded = (worst <= HGRN_MAX_EXPONENT).astype(jnp.int32).reshape(1)
    return pl.pallas_call(
        functools.partial(_hgrn_core_kernel, tm=tm),
        grid=(bsz, nt),
        in_specs=[pl.BlockSpec(memory_space=pltpu.SMEM),
                  pl.BlockSpec((tm, HGRN_COLS), row_map),
                  pl.BlockSpec((1, HGRN_DK), const2),
                  pl.BlockSpec((HGRN_CHUNK, HGRN_CHUNK), const2),
                  pl.BlockSpec((fd, D_MODEL), const2),
                  pl.BlockSpec((tm, D_MODEL), row_map)],
        out_specs=pl.BlockSpec((tm, D_MODEL), row_map),
        out_shape=jax.ShapeDtypeStruct((bsz * seq, D_MODEL), F32),
        scratch_shapes=[pltpu.VMEM((HGRN_HEADS, HGRN_DK, HGRN_DK), F32),
                        pltpu.VMEM((tm, fd), F32),
                        pltpu.VMEM((tm, fd), BF16)],
        compiler_params=_cparams("parallel", "arbitrary"),
        name="hgrn_core",
    )(bounded, prep, norm_w.reshape(1, HGRN_DK), tri, w_out_bf16, x2d)


def _ffn_kernel(x_ref, xh_ref, g_ref, wu_ref, cw_ref, cb_ref, wd_ref, o_ref, hn_scr, u_scr, act_scr, *, tm):
    fc = FFN_CHUNK
    first = pl.program_id(1) == 0
    g = g_ref[...]
    hn_scr[CONV_HALO:, :] = _rms_rows(x_ref[...], g).astype(BF16)
    hn_scr[:CONV_HALO, :] = jnp.where(first, 0.0, _rms_rows(xh_ref[...], g)).astype(BF16)

    def up(c):
        for half in range(2):
            cols = slice(half * D_FF + c * fc, half * D_FF + (c + 1) * fc)
            u_scr[c % 2, half] = _dot(hn_scr[...], wu_ref[:, cols])

    def gate(c):
        u = []
        for half in range(2):
            cols = slice(half * D_FF + c * fc, half * D_FF + (c + 1) * fc)
            u.append(_causal_conv(u_scr[c % 2, half], cw_ref.at[:, cols], cb_ref[:, cols], 3, tm))
        act_scr[:, c * fc:(c + 1) * fc] = (_silu(u[1]) * u[0]).astype(BF16)

    up(0)
    for c in range(N_FFN_CHUNKS):
        if c + 1 < N_FFN_CHUNKS:
            up(c + 1)
        gate(c)
    o_ref[...] = x_ref[...] + _dot(act_scr[...], wd_ref[...])


def _ffn(x2d, gain, w_up, conv_w, conv_b, w_down, bsz, seq, tm=512):
    nt = seq // tm
    hpt = tm // CONV_HALO
    fc = FFN_CHUNK
    wu = w_up.astype(BF16)
    wd = w_down.astype(BF16)
    row_map = lambda b, j: (b * nt + j, 0)
    halo_map = lambda b, j: (jnp.maximum((b * nt + j) * hpt - 1, 0), 0)
    const2 = lambda b, j: (0, 0)
    return pl.pallas_call(
        functools.partial(_ffn_kernel, tm=tm),
        grid=(bsz, nt),
        in_specs=[pl.BlockSpec((tm, D_MODEL), row_map),
                  pl.BlockSpec((CONV_HALO, D_MODEL), halo_map),
                  pl.BlockSpec((1, D_MODEL), const2),
                  pl.BlockSpec((D_MODEL, 2 * D_FF), const2),
                  pl.BlockSpec((3, 2 * D_FF), const2),
                  pl.BlockSpec((1, 2 * D_FF), const2),
                  pl.BlockSpec((D_FF, D_MODEL), const2)],
        out_specs=pl.BlockSpec((tm, D_MODEL), row_map),
        out_shape=jax.ShapeDtypeStruct((bsz * seq, D_MODEL), F32),
        scratch_shapes=[pltpu.VMEM((CONV_HALO + tm, D_MODEL), BF16),
                        pltpu.VMEM((2, 2, CONV_HALO + tm, fc), F32),
                        pltpu.VMEM((tm, D_FF), BF16)],
        compiler_params=_cparams("parallel", "parallel"),
        name="conv_ffn",
    )(x2d, x2d, gain.reshape(1, D_MODEL), wu, conv_w, conv_b.reshape(1, 2 * D_FF), wd)


def _lru_mixer(x2d, gain, w_in, conv_w, conv_b, w_gate, b_gate, lam, w_out, bsz, seq):
    coef = (-LRU_C * jax.nn.softplus(-lam.astype(F32))).reshape(1, D_RNN)
    params = [conv_w, conv_b.reshape(1, D_RNN), w_gate.astype(BF16), b_gate.reshape(1, 2 * D_RNN), coef]
    gelu, a, b = _norm_proj(x2d, gain, w_in.astype(BF16), params, [D_RNN] * 3, [F32] * 3, _lru_epilogue,
                            "lru_in", bsz, seq, width=LRU_WIDTH, halo=CONV_HALO)
    return _lru_scan(gelu, a, b, w_out.astype(BF16), x2d, bsz, seq)


def _ssd_mixer(x2d, gain, w_in, conv_w, conv_b, dt_bias, a_log, d_skip, norm_w, w_out, bsz, seq):
    pad = V7X_LANES - SSD_HEADS
    w_dt = jnp.pad(w_in[:, SSD_MAIN:], ((0, 0), (0, pad))).astype(BF16)
    dt_bias_pad = jnp.pad(dt_bias.astype(F32), (0, pad)).reshape(1, V7X_LANES)
    params = [conv_w, conv_b.reshape(1, SSD_CONV_DIM), w_dt, dt_bias_pad]
    zs, xbc, dt = _norm_proj(x2d, gain, w_in[:, :SSD_MAIN].astype(BF16), params,
                             [D_INNER, SSD_CONV_DIM, V7X_LANES], [F32] * 3, _ssd_epilogue, "ssd_in", bsz, seq,
                             width=SSD_WIDTH, halo=CONV_HALO, tail=_ssd_tail)
    a_pad = jnp.pad(-jnp.exp(a_log.astype(F32)), (0, pad)).reshape(1, V7X_LANES)
    d_exp = jnp.repeat(d_skip.astype(F32), SSD_HEAD_DIM).reshape(1, D_INNER)
    y = _ssd_core(zs, xbc, dt, a_pad, d_exp, norm_w, bsz, seq)
    return _out_proj(y, w_out.astype(BF16), x2d, "ssd_out")


def _swa_mixer(x2d, gain, w_in, q_norm, k_norm, sinks, w_out, bsz, seq):
    order = _swa_head_order()
    cols = (order[:, None] * SWA_HEAD_DIM + np.arange(SWA_HEAD_DIM)[None]).reshape(-1)
    w_in_perm = jnp.concatenate([w_in[:, cols], w_in[:, SWA_Q:]], axis=1).astype(BF16)
    q_gain = jnp.tile(q_norm.astype(F32), 2).reshape(1, V7X_LANES)
    k_gain = jnp.tile(k_norm.astype(F32), 2).reshape(1, V7X_LANES)
    half = np.arange(V7X_LANES) // SWA_HEAD_DIM
    bd = jnp.asarray(half[:, None] == half[None, :], BF16)
    qkv, = _norm_proj(x2d, gain, w_in_perm, [_rope_tables(seq), q_gain, k_gain, bd], [SWA_IN], [BF16],
                      _swa_epilogue, "swa_in", bsz, seq, width=SWA_WIDTH)
    o = _swa_core(sinks.astype(F32)[order], qkv, bsz, seq)
    return _out_proj(o, w_out[cols, :].astype(BF16), x2d, "swa_out")


def _hgrn_mixer(x2d, gain, w_in, norm_w, w_out, lower_bound, bsz, seq):
    prep, = _norm_proj(x2d, gain, w_in.astype(BF16), [lower_bound.reshape(1, HGRN_FD)], [HGRN_COLS], [F32],
                       _hgrn_epilogue, "hgrn_in", bsz, seq, width=HGRN_WIDTH)
    return _hgrn_core(prep, lower_bound, norm_w, w_out.astype(BF16), x2d, bsz, seq)


def kernel(x, mix_norm, ffn_norm, ffn_w_up, ffn_conv_w, ffn_conv_b, ffn_w_down, lru_w_in, lru_conv_w, lru_conv_b, lru_w_gate, lru_b_gate, lru_lambda, lru_w_out, ssd_w_in, ssd_conv_w, ssd_conv_b, ssd_dt_bias, ssd_a_log, ssd_d, ssd_norm, ssd_w_out, swa_w_in, swa_q_norm, swa_k_norm, swa_sinks, swa_w_out, hgrn_w_in, hgrn_norm, hgrn_w_out, hgrn_lower_bounds):
    bsz, seq, d = x.shape
    depth = mix_norm.shape[0]
    lbs = jnp.cumsum(jax.nn.softmax(hgrn_lower_bounds.astype(F32), axis=0), axis=0)
    lbs = lbs - lbs[0]
    h = x.reshape(bsz * seq, d)
    for i in range(depth):
        kind, j = i % 4, i // 4
        if kind == 0:
            h = _lru_mixer(h, mix_norm[i], lru_w_in[j], lru_conv_w[j], lru_conv_b[j], lru_w_gate[j],
                           lru_b_gate[j], lru_lambda[j], lru_w_out[j], bsz, seq)
        elif kind == 1:
            h = _ssd_mixer(h, mix_norm[i], ssd_w_in[j], ssd_conv_w[j], ssd_conv_b[j], ssd_dt_bias[j],
                           ssd_a_log[j], ssd_d[j], ssd_norm[j], ssd_w_out[j], bsz, seq)
        elif kind == 2:
            h = _swa_mixer(h, mix_norm[i], swa_w_in[j], swa_q_norm[j], swa_k_norm[j], swa_sinks[j],
                           swa_w_out[j], bsz, seq)
        else:
            h = _hgrn_mixer(h, mix_norm[i], hgrn_w_in[j], hgrn_norm[j], hgrn_w_out[j], lbs[i], bsz, seq)
        h = _ffn(h, ffn_norm[i], ffn_w_up[i], ffn_conv_w[i], ffn_conv_b[i], ffn_w_down[i], bsz, seq)
    return h.reshape(bsz, seq, d)
```

```python
import functools

import jax
import jax.numpy as jnp
import numpy as np
from jax import lax
from jax.experimental import pallas as pl
from jax.experimental.pallas import tpu as pltpu

F32 = jnp.float32
BF16 = jnp.bfloat16
EPS = 1e-6

V7X_LANES = 128
V7X_SUBLANES = 8
V7X_VMEM_BYTES = 64 * 1024 * 1024
VMEM_LIMIT_BYTES = V7X_VMEM_BYTES - 8 * 1024 * 1024

D_MODEL = 1024
D_FF = 2816
FFN_CHUNK = 256
N_FFN_CHUNKS = D_FF // FFN_CHUNK

D_RNN = 1280
LRU_BLOCKS = 10
LRU_BLK = 128
LRU_C = 8.0

D_INNER = 2048
SSD_HEADS = 32
SSD_HEAD_DIM = 64
SSD_GROUPS = 4
SSD_STATE = 128
SSD_CHUNK = 128
SSD_GN = SSD_GROUPS * SSD_STATE
SSD_CONV_DIM = D_INNER + 2 * SSD_GN
SSD_MAIN = D_INNER + SSD_CONV_DIM

SWA_HEAD_DIM = 64
SWA_Q_HEADS = 16
SWA_KV_HEADS = 4
SWA_BLOCK = 128
SWA_Q = SWA_Q_HEADS * SWA_HEAD_DIM
SWA_KV = SWA_KV_HEADS * SWA_HEAD_DIM
SWA_IN = SWA_Q + 2 * SWA_KV
ROT_DIM = 16
ROT_HALF = 8
ROPE_THETA = 500000.0

HGRN_HEADS = 8
HGRN_DK = 128
HGRN_CHUNK = 64
HGRN_SUB = 16
HGRN_TILE = 256

CONV_HALO = V7X_SUBLANES


def _cparams(*sem):
    return pltpu.CompilerParams(dimension_semantics=sem, vmem_limit_bytes=VMEM_LIMIT_BYTES)


def _sigmoid(x):
    return 1.0 / (1.0 + jnp.exp(-x))


def _silu(x):
    h = 0.5 * x
    return h + h * jnp.tanh(h)


def _rms_rows(x, g):
    ms = jnp.mean(x * x, axis=-1, keepdims=True)
    return x * lax.rsqrt(ms + EPS) * g


def _dot(a, b):
    return jnp.dot(a, b, preferred_element_type=F32)


def _dot_nt(a, b):
    return lax.dot_general(a, b, (((1,), (1,)), ((), ())), preferred_element_type=F32)


def _dot_exact_rhs(a, b01):
    a0 = a.astype(BF16)
    r1 = a - a0.astype(F32)
    a1 = r1.astype(BF16)
    a2 = (r1 - a1.astype(F32)).astype(BF16)
    return _dot(a0, b01) + _dot(a1, b01) + _dot(a2, b01)


def _dot_exact_lhs(a01, b):
    b0 = b.astype(BF16)
    r1 = b - b0.astype(F32)
    b1 = r1.astype(BF16)
    b2 = (r1 - b1.astype(F32)).astype(BF16)
    return _dot(a01, b0) + _dot(a01, b1) + _dot(a01, b2)


def _causal_conv(ext, w_ref, bias, taps):
    rows = ext.shape[0] - CONV_HALO
    acc = bias + w_ref[taps - 1:taps, :] * ext[:rows, :]
    for k in range(taps - 1):
        shift = taps - 1 - k
        acc = acc + w_ref[k:k + 1, :] * pltpu.roll(ext, shift, 0)[:rows, :]
    return acc


def _proj_kernel(x_ref, g_ref, w_ref, *rest, tm, halo, width, n_param, epilogue, tail):
    params = rest[:n_param]
    n_scr = 3 if halo else 2
    outs = rest[n_param:len(rest) - n_scr]
    hn_scr, u_scr = rest[len(rest) - n_scr:len(rest) - n_scr + 2]
    hn_scr[...] = _rms_rows(x_ref[...], g_ref[...]).astype(BF16)
    nc = w_ref.shape[1] // width
    if halo:
        prev_scr = rest[-1]
        first = pl.program_id(1) == 0

    def project(c):
        u_scr[c % 2, :tm] = _dot(hn_scr[...], w_ref[:, c * width:(c + 1) * width])
        if halo:
            u_scr[c % 2, tm:] = jnp.where(first, 0.0, prev_scr[c])
            prev_scr[c] = u_scr[c % 2, tm - halo:tm]

    project(0)
    for c in range(nc):
        if c + 1 < nc:
            project(c + 1)
        epilogue(c, u_scr.at[c % 2], params, outs)
    if tail is not None:
        tail(hn_scr, params, outs)


def _norm_proj(x2d, gain, w_bf16, params, out_widths, out_dtypes, epilogue, name, bsz, seq,
               tm=512, width=512, halo=0, tail=None):
    d = x2d.shape[1]
    nt = seq // tm
    row_map = lambda b, j: (b * nt + j, 0)
    whole = lambda a: pl.BlockSpec(a.shape, lambda b, j: (0,) * a.ndim)
    gain2 = gain.reshape(1, d)
    scratch = [pltpu.VMEM((tm, d), BF16), pltpu.VMEM((2, tm + halo, width), F32)]
    if halo:
        scratch.append(pltpu.VMEM((w_bf16.shape[1] // width, halo, width), F32))
    return pl.pallas_call(
        functools.partial(_proj_kernel, tm=tm, halo=halo, width=width, n_param=len(params),
                          epilogue=epilogue, tail=tail),
        grid=(bsz, nt),
        in_specs=[pl.BlockSpec((tm, d), row_map), whole(gain2), whole(w_bf16)] + [whole(p) for p in params],
        out_specs=[pl.BlockSpec((tm, w), row_map) for w in out_widths],
        out_shape=[jax.ShapeDtypeStruct((bsz * seq, w), dt) for w, dt in zip(out_widths, out_dtypes)],
        scratch_shapes=scratch,
        compiler_params=_cparams("parallel", "arbitrary" if halo else "parallel"),
        name=name,
    )(x2d, gain2, w_bf16, *params)


def _out_kernel(y_ref, w_ref, x_ref, o_ref):
    o_ref[...] = x_ref[...] + _dot(y_ref[...], w_ref[...])


def _out_proj(y_bf16, w_bf16, x2d, name, tm=512):
    t, k = y_bf16.shape
    d = w_bf16.shape[1]
    return pl.pallas_call(
        _out_kernel,
        grid=(t // tm,),
        in_specs=[pl.BlockSpec((tm, k), lambda i: (i, 0)),
                  pl.BlockSpec((k, d), lambda i: (0, 0)),
                  pl.BlockSpec((tm, d), lambda i: (i, 0))],
        out_specs=pl.BlockSpec((tm, d), lambda i: (i, 0)),
        out_shape=jax.ShapeDtypeStruct((t, d), F32),
        compiler_params=_cparams("parallel"),
        name=name,
    )(y_bf16, w_bf16, x2d)


LRU_WIDTH = 256


def _lru_epilogue(c, u_ref, params, outs):
    cw_ref, cb_ref, wg_ref, bg_ref, coef_ref = params
    gelu_ref, a_ref, b_ref = outs
    half = D_RNN // LRU_WIDTH
    if c < half:
        gate = u_ref[:u_ref.shape[0] - CONV_HALO, :]
        gelu_ref[:, c * LRU_WIDTH:(c + 1) * LRU_WIDTH] = 0.5 * gate * (
            1.0 + jnp.tanh(np.sqrt(2.0 / np.pi) * (gate + 0.044715 * (gate * gate * gate))))
        return
    cols = slice((c - half) * LRU_WIDTH, (c - half + 1) * LRU_WIDTH)
    xc = _causal_conv(u_ref[...], cw_ref.at[:, cols], cb_ref[:, cols], 4)
    for k in range(LRU_WIDTH // LRU_BLK):
        n = (c - half) * (LRU_WIDTH // LRU_BLK) + k
        lo, hi = n * LRU_BLK, (n + 1) * LRU_BLK
        xcn = xc[:, k * LRU_BLK:(k + 1) * LRU_BLK]
        g = _dot(xcn.astype(BF16), wg_ref[n]) + bg_ref[:, 2 * lo:2 * hi]
        r = _sigmoid(g[:, :LRU_BLK])
        i = _sigmoid(g[:, LRU_BLK:])
        log_a = coef_ref[:, lo:hi] * r
        a = jnp.exp(log_a)
        a_ref[:, lo:hi] = a
        b_ref[:, lo:hi] = jnp.sqrt(-jnp.tanh(log_a) * (a * a + 1.0)) * (i * xcn)


def _lru_scan_kernel(gelu_ref, a_ref, b_ref, w_ref, x_ref, o_ref, h_ref, hs_scr, *, tm):
    @pl.when(pl.program_id(1) == 0)
    def _():
        h_ref[...] = jnp.zeros_like(h_ref)

    row = lax.broadcasted_iota(jnp.int32, (V7X_SUBLANES, D_RNN), 0)

    def scan_rows(t, h):
        r0 = pl.multiple_of(t * V7X_SUBLANES, V7X_SUBLANES)
        a = a_ref[pl.ds(r0, V7X_SUBLANES), :]
        u = b_ref[pl.ds(r0, V7X_SUBLANES), :]
        for s in (1, 2, 4):
            keep = row >= s
            a_prev = jnp.where(keep, pltpu.roll(a, s, 0), 1.0)
            u_prev = jnp.where(keep, pltpu.roll(u, s, 0), 0.0)
            u = a * u_prev + u
            a = a * a_prev
        hs = a * h + u
        hs_scr[pl.ds(r0, V7X_SUBLANES), :] = hs
        return hs[V7X_SUBLANES - 1:, :]

    h_ref[...] = lax.fori_loop(0, tm // V7X_SUBLANES, scan_rows, h_ref[...], unroll=2)
    y = (gelu_ref[...] * hs_scr[...]).astype(BF16)
    o_ref[...] = x_ref[...] + _dot(y, w_ref[...])


def _lru_scan(gelu, a, b, w_out_bf16, x2d, bsz, seq, tm=256):
    nt = seq // tm
    row_map = lambda bb, j: (bb * nt + j, 0)
    spec = pl.BlockSpec((tm, D_RNN), row_map)
    xspec = pl.BlockSpec((tm, D_MODEL), row_map)
    return pl.pallas_call(
        functools.partial(_lru_scan_kernel, tm=tm),
        grid=(bsz, nt),
        in_specs=[spec, spec, spec, pl.BlockSpec((D_RNN, D_MODEL), lambda bb, j: (0, 0)), xspec],
        out_specs=xspec,
        out_shape=jax.ShapeDtypeStruct((bsz * seq, D_MODEL), F32),
        scratch_shapes=[pltpu.VMEM((1, D_RNN), F32),
                        pltpu.VMEM((tm, D_RNN), F32)],
        compiler_params=_cparams("parallel", "arbitrary"),
        name="lru_scan",
    )(gelu, a, b, w_out_bf16, x2d)


SSD_WIDTH = 512


def _softplus(x):
    return jnp.maximum(x, 0.0) + jnp.log1p(jnp.exp(-jnp.abs(x)))


def _ssd_epilogue(c, u_ref, params, outs):
    cw_ref, cb_ref = params[:2]
    zs_ref, xbc_ref = outs[:2]
    nz = D_INNER // SSD_WIDTH
    if c < nz:
        zs_ref[:, c * SSD_WIDTH:(c + 1) * SSD_WIDTH] = _silu(u_ref[:u_ref.shape[0] - CONV_HALO, :])
    else:
        cols = slice((c - nz) * SSD_WIDTH, (c - nz + 1) * SSD_WIDTH)
        xbc_ref[:, cols] = _silu(_causal_conv(u_ref[...], cw_ref.at[:, cols], cb_ref[:, cols], 4))


def _ssd_tail(hn_scr, params, outs):
    wdt_ref, dtb_ref = params[2:]
    outs[2][...] = _softplus(_dot(hn_scr[...], wdt_ref[...]) + dtb_ref[...])


def _ssd_core_kernel(zs_ref, xbc_ref, dt_ref, a_ref, dskip_ref, nw_ref, tri_ref, y_ref, st_ref):
    L = SSD_CHUNK
    hpg = SSD_HEADS // SSD_GROUPS

    @pl.when(pl.program_id(1) == 0)
    def _():
        st_ref[...] = jnp.zeros_like(st_ref)

    dt = dt_ref[...]
    cs = _dot_exact_lhs(tri_ref[...], dt * a_ref[...])
    cs_t = cs.T
    dt_t = dt.T

    ri = lax.broadcasted_iota(jnp.int32, (L, L), 0)
    ci = lax.broadcasted_iota(jnp.int32, (L, L), 1)
    causal = ri >= ci
    low_lanes = ci < SSD_HEAD_DIM

    bms = [xbc_ref[:, D_INNER + g * SSD_STATE:D_INNER + (g + 1) * SSD_STATE] for g in range(SSD_GROUPS)]
    cms = [xbc_ref[:, D_INNER + SSD_GN + g * SSD_STATE:D_INNER + SSD_GN + (g + 1) * SSD_STATE]
           for g in range(SSD_GROUPS)]
    cbs = [_dot_nt(cms[g].astype(BF16), bms[g].astype(BF16)) for g in range(SSD_GROUPS)]
    bm_ts = [bm.T for bm in bms]

    for g in range(SSD_GROUPS):
        heads = range(g * hpg, (g + 1) * hpg)
        lhs_y, lhs_s, decs = {}, {}, {}
        for h in heads:
            col = jnp.broadcast_to(cs[:, h:h + 1], (L, L))
            rowv = cs_t[h:h + 1, :]
            dt_row = dt_t[h:h + 1, :]
            last = cs_t[h:h + 1, L - 1:L]
            decay = jnp.exp(jnp.where(causal, col - rowv, -jnp.inf))
            w_diag = (cbs[g] * decay * dt_row).astype(BF16)
            c_off = (cms[g] * jnp.exp(col)).astype(BF16)
            lhs_y[h] = jnp.concatenate([w_diag, c_off], axis=1)
            lhs_s[h] = (bm_ts[g] * (jnp.exp(last - rowv) * dt_row)).astype(BF16)
            decs[h] = jnp.exp(last)
        rhs, xs_bf = {}, {}
        for pair in range(g * hpg // 2, (g + 1) * hpg // 2):
            lanes = slice(pair * V7X_LANES, (pair + 1) * V7X_LANES)
            xs_bf[pair] = xbc_ref[:, lanes].astype(BF16)
            rhs[pair] = jnp.concatenate([xs_bf[pair], st_ref[:, lanes].astype(BF16)], axis=0)
        ys = {h: _dot(lhs_y[h], rhs[h // 2]) for h in heads}
        sts = {h: _dot(lhs_s[h], xs_bf[h // 2]) for h in heads}
        y_pairs = []
        for pair in range(g * hpg // 2, (g + 1) * hpg // 2):
            lanes = slice(pair * V7X_LANES, (pair + 1) * V7X_LANES)
            h0, h1 = 2 * pair, 2 * pair + 1
            y_pairs.append(jnp.where(low_lanes, ys[h0], ys[h1]))
            dec = jnp.where(low_lanes[:1, :], decs[h0], decs[h1])
            st_ref[:, lanes] = st_ref[:, lanes] * dec + jnp.where(low_lanes, sts[h0], sts[h1])
        glanes = slice(g * 512, (g + 1) * 512)
        y = jnp.concatenate(y_pairs, axis=1) + xbc_ref[:, glanes] * dskip_ref[:, glanes]
        y = y * zs_ref[:, glanes]
        ms = jnp.mean(y * y, axis=-1, keepdims=True)
        y_ref[:, glanes] = (y * lax.rsqrt(ms + EPS) * nw_ref[:, glanes]).astype(BF16)


def _ssd_core(zs, xbc, dt, a_pad, d_skip_exp, norm_w, bsz, seq):
    L = SSD_CHUNK
    nt = seq // L
    row_map = lambda b, j: (b * nt + j, 0)
    const2 = lambda b, j: (0, 0)
    tri = jnp.tril(jnp.ones((L, L), F32)).astype(BF16)
    return pl.pallas_call(
        _ssd_core_kernel,
        grid=(bsz, nt),
        in_specs=[pl.BlockSpec((L, D_INNER), row_map),
                  pl.BlockSpec((L, SSD_CONV_DIM), row_map),
                  pl.BlockSpec((L, V7X_LANES), row_map),
                  pl.BlockSpec((1, V7X_LANES), const2),
                  pl.BlockSpec((1, D_INNER), const2),
                  pl.BlockSpec((1, D_INNER), const2),
                  pl.BlockSpec((L, L), const2)],
        out_specs=pl.BlockSpec((L, D_INNER), row_map),
        out_shape=jax.ShapeDtypeStruct((bsz * seq, D_INNER), BF16),
        scratch_shapes=[pltpu.VMEM((SSD_STATE, D_INNER), F32)],
        compiler_params=_cparams("parallel", "arbitrary"),
        name="ssd_core",
    )(zs, xbc, dt, a_pad, d_skip_exp, norm_w.reshape(1, D_INNER), tri)


SWA_WIDTH = 512


def _swa_epilogue(c, u_ref, params, outs):
    rope_ref, qn_ref, kn_ref, bd_ref = params
    out_ref = outs[0]
    tm = u_ref.shape[0]
    r0 = pl.multiple_of(pl.program_id(1) * tm, tm)
    rope = [rope_ref[k, pl.ds(r0, tm), :] for k in range(3)]
    for t in range(SWA_WIDTH // V7X_LANES):
        col = c * SWA_WIDTH + t * V7X_LANES
        x = u_ref[:, t * V7X_LANES:(t + 1) * V7X_LANES]
        if col < SWA_Q + SWA_KV:
            is_q = col < SWA_Q
            sq = x * x
            s0 = sq.astype(BF16)
            s1 = (sq - s0.astype(F32)).astype(BF16)
            ss = _dot(s0, bd_ref[...]) + _dot(s1, bd_ref[...])
            xn = x * lax.rsqrt(ss * (1.0 / SWA_HEAD_DIM) + EPS) * (qn_ref[...] if is_q else kn_ref[...])
            x = (xn * rope[0] + pltpu.roll(xn, V7X_LANES - ROT_HALF, 1) * rope[1]
                 + pltpu.roll(xn, ROT_HALF, 1) * rope[2])
            if is_q:
                x = x * (SWA_HEAD_DIM ** -0.5)
        out_ref[:, col:col + V7X_LANES] = x.astype(BF16)


def _swa_core_kernel(sink_ref, p_ref, pkv_ref, o_ref):
    blk = SWA_BLOCK
    first_key = jnp.where(pl.program_id(1) == 0, blk, 0)
    qi = lax.broadcasted_iota(jnp.int32, (blk, 2 * blk), 0)
    kj = lax.broadcasted_iota(jnp.int32, (blk, 2 * blk), 1)
    window = (kj > qi) & (kj <= qi + blk) & (kj >= first_key)
    lane = lax.broadcasted_iota(jnp.int32, (blk, V7X_LANES), 1)
    low = lane < SWA_HEAD_DIM
    n_pairs = SWA_Q // V7X_LANES

    kws, vws = [], []
    for kp in range(SWA_KV_HEADS // 2):
        kws.append(jnp.concatenate([pkv_ref[:, kp * V7X_LANES:(kp + 1) * V7X_LANES],
                                    p_ref[:, SWA_Q + kp * V7X_LANES:SWA_Q + (kp + 1) * V7X_LANES]], axis=0))
        vws.append(jnp.concatenate([pkv_ref[:, SWA_KV + kp * V7X_LANES:SWA_KV + (kp + 1) * V7X_LANES],
                                    p_ref[:, SWA_Q + SWA_KV + kp * V7X_LANES:
                                          SWA_Q + SWA_KV + (kp + 1) * V7X_LANES]], axis=0))
    half_masks = [jnp.where(lane[:1, :] < SWA_HEAD_DIM, 1.0, 0.0).astype(BF16),
                  jnp.where(lane[:1, :] < SWA_HEAD_DIM, 0.0, 1.0).astype(BF16)]
    kws = [[kw * hm for hm in half_masks] for kw in kws]
    scores = []
    for pair in range(n_pairs):
        q = p_ref[:, pair * V7X_LANES:(pair + 1) * V7X_LANES]
        for half in range(2):
            scores.append(jnp.where(window, _dot_nt(q, kws[pair // 4][half]), -jnp.inf))
    probs, denoms = [], []
    for i, s in enumerate(scores):
        sink = sink_ref[i]
        m = jnp.maximum(jnp.max(s, axis=-1, keepdims=True), sink)
        p = jnp.exp(s - m)
        denoms.append(jnp.sum(p, axis=-1, keepdims=True) + jnp.exp(sink - m))
        probs.append(p.astype(BF16))
    outs = [_dot(probs[i], vws[i // 8]) / denoms[i] for i in range(2 * n_pairs)]
    for pair in range(n_pairs):
        o_ref[:, pair * V7X_LANES:(pair + 1) * V7X_LANES] = jnp.where(
            low, outs[2 * pair], outs[2 * pair + 1]).astype(BF16)


def _swa_core(sinks_perm, qkv, bsz, seq):
    blk = SWA_BLOCK
    nt = seq // blk
    row_map = lambda b, j: (b * nt + j, 0)
    prev_map = lambda b, j: (jnp.maximum(b * nt + j - 1, 0), SWA_Q // (2 * SWA_KV))
    return pl.pallas_call(
        _swa_core_kernel,
        grid=(bsz, nt),
        in_specs=[pl.BlockSpec(memory_space=pltpu.SMEM),
                  pl.BlockSpec((blk, SWA_IN), row_map),
                  pl.BlockSpec((blk, 2 * SWA_KV), prev_map)],
        out_specs=pl.BlockSpec((blk, SWA_Q), row_map),
        out_shape=jax.ShapeDtypeStruct((bsz * seq, SWA_Q), BF16),
        compiler_params=_cparams("parallel", "parallel"),
        name="swa_core",
    )(sinks_perm, qkv, qkv)


def _swa_head_order():
    order = []
    for kp in range(SWA_KV_HEADS // 2):
        for i in range(4):
            order += [4 * (2 * kp) + i, 4 * (2 * kp + 1) + i]
    return np.asarray(order)


def _rope_tables(seq):
    inv_freq = ROPE_THETA ** (-np.arange(0, ROT_DIM, 2, dtype=np.float32) / ROT_DIM)
    ang = np.arange(seq, dtype=np.float32)[:, None] * inv_freq[None].astype(np.float32)
    cos, sin = np.cos(ang), np.sin(ang)
    ones = np.ones((seq, SWA_HEAD_DIM - ROT_DIM), np.float32)
    zeros = np.zeros((seq, SWA_HEAD_DIM - ROT_DIM), np.float32)
    z8 = np.zeros((seq, ROT_HALF), np.float32)
    c = np.concatenate([cos, cos, ones], axis=1)
    sa = np.concatenate([-sin, z8, zeros], axis=1)
    sb = np.concatenate([z8, sin, zeros], axis=1)
    tab = np.stack([c, sa, sb])
    return jnp.asarray(np.concatenate([tab, tab], axis=2), F32)


HGRN_WIDTH = 512
HGRN_FD = HGRN_HEADS * HGRN_DK
HGRN_COLS = 5 * HGRN_FD


def _hgrn_epilogue(c, u_ref, params, outs):
    lb_ref = params[0]
    out_ref = outs[0]
    kind, k = divmod(c, HGRN_FD // HGRN_WIDTH)
    cols = slice(k * HGRN_WIDTH, (k + 1) * HGRN_WIDTH)
    u = u_ref[...]
    if kind == 0:
        out_ref[:, cols] = u * (HGRN_DK ** -0.5)
    elif kind == 1:
        lb = lb_ref[:, cols]
        f = lb + (1.0 - lb) * _sigmoid(u)
        out_ref[:, HGRN_FD + k * HGRN_WIDTH:HGRN_FD + (k + 1) * HGRN_WIDTH] = jnp.log(f)
        out_ref[:, 2 * HGRN_FD + k * HGRN_WIDTH:2 * HGRN_FD + (k + 1) * HGRN_WIDTH] = 1.0 - f
    elif kind == 2:
        out_ref[:, 3 * HGRN_FD + k * HGRN_WIDTH:3 * HGRN_FD + (k + 1) * HGRN_WIDTH] = u
    else:
        out_ref[:, 4 * HGRN_FD + k * HGRN_WIDTH:4 * HGRN_FD + (k + 1) * HGRN_WIDTH] = _silu(u)


def _hgrn_chunk_exact(h, r0, q, v, kk, cum, st_ref, o_scr, sub_row):
    L, C = HGRN_CHUNK, HGRN_SUB
    lanes = slice(h * HGRN_DK, (h + 1) * HGRN_DK)
    total = cum[L - 1:, :]
    st = st_ref[h]
    v_bf = v.astype(BF16)
    o = _dot_nt((q * jnp.exp(cum)).astype(BF16), st.astype(BF16))
    o_sub = []
    for i in range(L // C):
        qi = q[i * C:(i + 1) * C]
        cumi = cum[i * C:(i + 1) * C]
        oi = o[i * C:(i + 1) * C]
        if i > 0:
            ref = cum[i * C - 1:i * C]
            q_t = (qi * jnp.exp(cumi - ref)).astype(BF16)
            k_t = (kk[:i * C] * jnp.exp(ref - cum[:i * C])).astype(BF16)
            attn = _dot_nt(q_t, k_t)
            oi = oi + _dot(attn.astype(BF16), v_bf[:i * C])
        for s in range(C):
            r = i * C + s
            e = jnp.exp(jnp.where(sub_row >= s, cumi - cum[r:r + 1], -jnp.inf))
            w = jnp.sum(qi * kk[r:r + 1] * e, axis=-1, keepdims=True)
            oi = oi + w * v[r:r + 1]
        o_sub.append(oi)
    o_scr[pl.ds(r0, L), lanes] = jnp.concatenate(o_sub, axis=0)
    k_end = (kk * jnp.exp(total - cum)).astype(BF16)
    st_ref[h] = st * jnp.exp(total) + lax.dot_general(
        v_bf, k_end, (((0,), (0,)), ((), ())), preferred_element_type=F32)


def _hgrn_chunk_bounded(r0, qs, vs, kks, cums, st_ref, o_scr, causal):
    L = HGRN_CHUNK
    heads = range(HGRN_HEADS)
    mids = [c[L // 2 - 1:L // 2, :] for c in cums]
    totals = [c[L - 1:, :] for c in cums]
    sts = [st_ref[h] for h in heads]
    v_bf = [v.astype(BF16) for v in vs]
    qe = [qs[h] * jnp.exp(cums[h] - mids[h]) for h in heads]
    ke = [kks[h] * jnp.exp(mids[h] - cums[h]) for h in heads]
    attn = [jnp.where(causal, _dot_nt(qe[h].astype(BF16), ke[h].astype(BF16)), 0.0).astype(BF16) for h in heads]
    o_inter = [_dot_nt((qe[h] * jnp.exp(mids[h])).astype(BF16), sts[h].astype(BF16)) for h in heads]
    o_intra = [_dot(attn[h], v_bf[h]) for h in heads]
    upd = [lax.dot_general(v_bf[h], (ke[h] * jnp.exp(totals[h] - mids[h])).astype(BF16),
                           (((0,), (0,)), ((), ())), preferred_element_type=F32) for h in heads]
    for h in heads:
        o_scr[pl.ds(r0, L), h * HGRN_DK:(h + 1) * HGRN_DK] = o_intra[h] + o_inter[h]
        st_ref[h] = sts[h] * jnp.exp(totals[h]) + upd[h]


def _hgrn_core_kernel(bounded_ref, p_ref, nw_ref, tri_ref, w_ref, x_ref, out_ref, st_ref, o_scr, y_scr, *, tm):
    L, C = HGRN_CHUNK, HGRN_SUB
    fd = HGRN_FD

    @pl.when(pl.program_id(1) == 0)
    def _():
        st_ref[...] = jnp.zeros_like(st_ref)

    sub_row = lax.broadcasted_iota(jnp.int32, (C, HGRN_DK), 0)
    causal = lax.broadcasted_iota(jnp.int32, (L, L), 0) >= lax.broadcasted_iota(jnp.int32, (L, L), 1)
    tri = tri_ref[...]

    def operands(h, r0):
        lanes = slice(h * HGRN_DK, (h + 1) * HGRN_DK)
        q = p_ref[pl.ds(r0, L), lanes]
        v = p_ref[pl.ds(r0, L), 3 * fd + h * HGRN_DK:3 * fd + (h + 1) * HGRN_DK]
        kk = p_ref[pl.ds(r0, L), 2 * fd + h * HGRN_DK:2 * fd + (h + 1) * HGRN_DK]
        cum = _dot_exact_lhs(tri, p_ref[pl.ds(r0, L), fd + h * HGRN_DK:fd + (h + 1) * HGRN_DK])
        return q, v, kk, cum

    def bounded_chunk(c):
        r0 = c * L
        cum_all = _dot_exact_lhs(tri, p_ref[pl.ds(r0, L), fd:2 * fd])
        qs, vs, kks, cums = [], [], [], []
        for h in range(HGRN_HEADS):
            lanes = slice(h * HGRN_DK, (h + 1) * HGRN_DK)
            qs.append(p_ref[pl.ds(r0, L), lanes])
            vs.append(p_ref[pl.ds(r0, L), 3 * fd + h * HGRN_DK:3 * fd + (h + 1) * HGRN_DK])
            kks.append(p_ref[pl.ds(r0, L), 2 * fd + h * HGRN_DK:2 * fd + (h + 1) * HGRN_DK])
            cums.append(cum_all[:, lanes])
        _hgrn_chunk_bounded(r0, qs, vs, kks, cums, st_ref, o_scr, causal)

    def exact_chunk(c, carry):
        r0 = pl.multiple_of(c * L, L)
        for h in range(HGRN_HEADS):
            _hgrn_chunk_exact(h, r0, *operands(h, r0), st_ref, o_scr, sub_row)
        return carry

    @pl.when(bounded_ref[0] == 1)
    def _():
        for c in range(tm // L):
            bounded_chunk(c)

    @pl.when(bounded_ref[0] == 0)
    def _():
        lax.fori_loop(0, tm // L, exact_chunk, 0)

    for h in range(HGRN_HEADS):
        lanes = slice(h * HGRN_DK, (h + 1) * HGRN_DK)
        o = _rms_rows(o_scr[:, lanes], nw_ref[...])
        y_scr[:, lanes] = (o * p_ref[:, 4 * fd + h * HGRN_DK:4 * fd + (h + 1) * HGRN_DK]).astype(BF16)
    out_ref[...] = x_ref[...] + _dot(y_scr[...], w_ref[...])


HGRN_MAX_EXPONENT = 80.0


def _hgrn_core(prep, lower_bound, norm_w, w_out_bf16, x2d, bsz, seq, tm=HGRN_TILE):
    nt = seq // tm
    fd = HGRN_FD
    row_map = lambda b, j: (b * nt + j, 0)
    const2 = lambda b, j: (0, 0)
    tri = jnp.tril(jnp.ones((HGRN_CHUNK, HGRN_CHUNK), F32)).astype(BF16)
    worst = -(HGRN_CHUNK // 2) * jnp.log(jnp.min(lower_bound))
    bounded = (worst <= HGRN_MAX_EXPONENT).astype(jnp.int32).reshape(1)
    return pl.pallas_call(
        functools.partial(_hgrn_core_kernel, tm=tm),
        grid=(bsz, nt),
        in_specs=[pl.BlockSpec(memory_space=pltpu.SMEM),
                  pl.BlockSpec((tm, HGRN_COLS), row_map),
                  pl.BlockSpec((1, HGRN_DK), const2),
                  pl.BlockSpec((HGRN_CHUNK, HGRN_CHUNK), const2),
                  pl.BlockSpec((fd, D_MODEL), const2),
                  pl.BlockSpec((tm, D_MODEL), row_map)],
        out_specs=pl.BlockSpec((tm, D_MODEL), row_map),
        out_shape=jax.ShapeDtypeStruct((bsz * seq, D_MODEL), F32),
        scratch_shapes=[pltpu.VMEM((HGRN_HEADS, HGRN_DK, HGRN_DK), F32),
                        pltpu.VMEM((tm, fd), F32),
                        pltpu.VMEM((tm, fd), BF16)],
        compiler_params=_cparams("parallel", "arbitrary"),
        name="hgrn_core",
    )(bounded, prep, norm_w.reshape(1, HGRN_DK), tri, w_out_bf16, x2d)


def _ffn_kernel(x_ref, g_ref, wu_ref, cw_ref, cb_ref, wd_ref, o_ref, hn_scr, u_scr, act_scr, prev_scr, *, tm):
    fc = FFN_CHUNK
    first = pl.program_id(1) == 0
    hn_scr[...] = _rms_rows(x_ref[...], g_ref[...]).astype(BF16)

    def up(c):
        for half in range(2):
            cols = slice(half * D_FF + c * fc, half * D_FF + (c + 1) * fc)
            u_scr[c % 2, half, :tm] = _dot(hn_scr[...], wu_ref[:, cols])
            u_scr[c % 2, half, tm:] = jnp.where(first, 0.0, prev_scr[c, half])
            prev_scr[c, half] = u_scr[c % 2, half, tm - CONV_HALO:tm]

    def gate(c):
        u = []
        for half in range(2):
            cols = slice(half * D_FF + c * fc, half * D_FF + (c + 1) * fc)
            u.append(_causal_conv(u_scr[c % 2, half], cw_ref.at[:, cols], cb_ref[:, cols], 3))
        act_scr[:, c * fc:(c + 1) * fc] = (_silu(u[1]) * u[0]).astype(BF16)

    up(0)
    for c in range(N_FFN_CHUNKS):
        if c + 1 < N_FFN_CHUNKS:
            up(c + 1)
        gate(c)
    o_ref[...] = x_ref[...] + _dot(act_scr[...], wd_ref[...])


def _ffn(x2d, gain, w_up_bf16, conv_w, conv_b, w_down_bf16, layer, bsz, seq, tm=512):
    nt = seq // tm
    fc = FFN_CHUNK
    row_map = lambda b, j: (b * nt + j, 0)
    const2 = lambda b, j: (0, 0)
    of_layer = lambda b, j: (layer, 0, 0)
    return pl.pallas_call(
        functools.partial(_ffn_kernel, tm=tm),
        grid=(bsz, nt),
        in_specs=[pl.BlockSpec((tm, D_MODEL), row_map),
                  pl.BlockSpec((1, D_MODEL), const2),
                  pl.BlockSpec((None, D_MODEL, 2 * D_FF), of_layer),
                  pl.BlockSpec((None, 3, 2 * D_FF), of_layer),
                  pl.BlockSpec((None, 1, 2 * D_FF), of_layer),
                  pl.BlockSpec((None, D_FF, D_MODEL), of_layer)],
        out_specs=pl.BlockSpec((tm, D_MODEL), row_map),
        out_shape=jax.ShapeDtypeStruct((bsz * seq, D_MODEL), F32),
        scratch_shapes=[pltpu.VMEM((tm, D_MODEL), BF16),
                        pltpu.VMEM((2, 2, tm + CONV_HALO, fc), F32),
                        pltpu.VMEM((tm, D_FF), BF16),
                        pltpu.VMEM((N_FFN_CHUNKS, 2, CONV_HALO, fc), F32)],
        compiler_params=_cparams("parallel", "arbitrary"),
        name="conv_ffn",
    )(x2d, gain.reshape(1, D_MODEL), w_up_bf16, conv_w, conv_b.reshape(-1, 1, 2 * D_FF), w_down_bf16)


def _lru_mixer(x2d, gain, w_in, conv_w, conv_b, w_gate, b_gate, lam, w_out, bsz, seq):
    coef = (-LRU_C * jax.nn.softplus(-lam.astype(F32))).reshape(1, D_RNN)
    params = [conv_w, conv_b.reshape(1, D_RNN), w_gate.astype(BF16), b_gate.reshape(1, 2 * D_RNN), coef]
    gelu, a, b = _norm_proj(x2d, gain, w_in.astype(BF16), params, [D_RNN] * 3, [F32] * 3, _lru_epilogue,
                            "lru_in", bsz, seq, width=LRU_WIDTH, halo=CONV_HALO)
    return _lru_scan(gelu, a, b, w_out.astype(BF16), x2d, bsz, seq)


def _ssd_mixer(x2d, gain, w_in, conv_w, conv_b, dt_bias, a_log, d_skip, norm_w, w_out, bsz, seq):
    pad = V7X_LANES - SSD_HEADS
    w_dt = jnp.pad(w_in[:, SSD_MAIN:], ((0, 0), (0, pad))).astype(BF16)
    dt_bias_pad = jnp.pad(dt_bias.astype(F32), (0, pad)).reshape(1, V7X_LANES)
    params = [conv_w, conv_b.reshape(1, SSD_CONV_DIM), w_dt, dt_bias_pad]
    zs, xbc, dt = _norm_proj(x2d, gain, w_in[:, :SSD_MAIN].astype(BF16), params,
                             [D_INNER, SSD_CONV_DIM, V7X_LANES], [F32] * 3, _ssd_epilogue, "ssd_in", bsz, seq,
                             width=SSD_WIDTH, halo=CONV_HALO, tail=_ssd_tail)
    a_pad = jnp.pad(-jnp.exp(a_log.astype(F32)), (0, pad)).reshape(1, V7X_LANES)
    d_exp = jnp.repeat(d_skip.astype(F32), SSD_HEAD_DIM).reshape(1, D_INNER)
    y = _ssd_core(zs, xbc, dt, a_pad, d_exp, norm_w, bsz, seq)
    return _out_proj(y, w_out.astype(BF16), x2d, "ssd_out")


def _swa_mixer(x2d, gain, w_in, q_norm, k_norm, sinks, w_out, bsz, seq):
    order = _swa_head_order()
    cols = (order[:, None] * SWA_HEAD_DIM + np.arange(SWA_HEAD_DIM)[None]).reshape(-1)
    w_in_perm = jnp.concatenate([w_in[:, cols], w_in[:, SWA_Q:]], axis=1).astype(BF16)
    q_gain = jnp.tile(q_norm.astype(F32), 2).reshape(1, V7X_LANES)
    k_gain = jnp.tile(k_norm.astype(F32), 2).reshape(1, V7X_LANES)
    half = np.arange(V7X_LANES) // SWA_HEAD_DIM
    bd = jnp.asarray(half[:, None] == half[None, :], BF16)
    qkv, = _norm_proj(x2d, gain, w_in_perm, [_rope_tables(seq), q_gain, k_gain, bd], [SWA_IN], [BF16],
                      _swa_epilogue, "swa_in", bsz, seq, width=SWA_WIDTH)
    o = _swa_core(sinks.astype(F32)[order], qkv, bsz, seq)
    return _out_proj(o, w_out[cols, :].astype(BF16), x2d, "swa_out")


def _hgrn_mixer(x2d, gain, w_in, norm_w, w_out, lower_bound, bsz, seq):
    prep, = _norm_proj(x2d, gain, w_in.astype(BF16), [lower_bound.reshape(1, HGRN_FD)], [HGRN_COLS], [F32],
                       _hgrn_epilogue, "hgrn_in", bsz, seq, width=HGRN_WIDTH)
    return _hgrn_core(prep, lower_bound, norm_w, w_out.astype(BF16), x2d, bsz, seq)


def kernel(x, mix_norm, ffn_norm, ffn_w_up, ffn_conv_w, ffn_conv_b, ffn_w_down, lru_w_in, lru_conv_w, lru_conv_b, lru_w_gate, lru_b_gate, lru_lambda, lru_w_out, ssd_w_in, ssd_conv_w, ssd_conv_b, ssd_dt_bias, ssd_a_log, ssd_d, ssd_norm, ssd_w_out, swa_w_in, swa_q_norm, swa_k_norm, swa_sinks, swa_w_out, hgrn_w_in, hgrn_norm, hgrn_w_out, hgrn_lower_bounds):
    bsz, seq, d = x.shape
    depth = mix_norm.shape[0]
    lbs = jnp.cumsum(jax.nn.softmax(hgrn_lower_bounds.astype(F32), axis=0), axis=0)
    lbs = lbs - lbs[0]
    ffn_w_up_bf16 = ffn_w_up.astype(BF16)
    ffn_w_down_bf16 = ffn_w_down.astype(BF16)
    h = x.reshape(bsz * seq, d)
    for i in range(depth):
        kind, j = i % 4, i // 4
        if kind == 0:
            h = _lru_mixer(h, mix_norm[i], lru_w_in[j], lru_conv_w[j], lru_conv_b[j], lru_w_gate[j],
                           lru_b_gate[j], lru_lambda[j], lru_w_out[j], bsz, seq)
        elif kind == 1:
            h = _ssd_mixer(h, mix_norm[i], ssd_w_in[j], ssd_conv_w[j], ssd_conv_b[j], ssd_dt_bias[j],
                           ssd_a_log[j], ssd_d[j], ssd_norm[j], ssd_w_out[j], bsz, seq)
        elif kind == 2:
            h = _swa_mixer(h, mix_norm[i], swa_w_in[j], swa_q_norm[j], swa_k_norm[j], swa_sinks[j],
                           swa_w_out[j], bsz, seq)
        else:
            h = _hgrn_mixer(h, mix_norm[i], hgrn_w_in[j], hgrn_norm[j], hgrn_w_out[j], lbs[i], bsz, seq)
        h = _ffn(h, ffn_norm[i], ffn_w_up_bf16, ffn_conv_w, ffn_conv_b, ffn_w_down_bf16, i, bsz, seq)
    return h.reshape(bsz, seq, d)
```

```python
import functools

import jax
import jax.numpy as jnp
import numpy as np
from jax import lax
from jax.experimental import pallas as pl
from jax.experimental.pallas import tpu as pltpu

F32 = jnp.float32
BF16 = jnp.bfloat16
EPS = 1e-6

V7X_LANES = 128
V7X_SUBLANES = 8
V7X_VMEM_BYTES = 64 * 1024 * 1024
VMEM_LIMIT_BYTES = V7X_VMEM_BYTES - 8 * 1024 * 1024

D_MODEL = 1024
D_FF = 2816
FFN_CHUNK = 256
N_FFN_CHUNKS = D_FF // FFN_CHUNK

D_RNN = 1280
LRU_BLOCKS = 10
LRU_BLK = 128
LRU_C = 8.0

D_INNER = 2048
SSD_HEADS = 32
SSD_HEAD_DIM = 64
SSD_GROUPS = 4
SSD_STATE = 128
SSD_CHUNK = 128
SSD_GN = SSD_GROUPS * SSD_STATE
SSD_CONV_DIM = D_INNER + 2 * SSD_GN
SSD_MAIN = D_INNER + SSD_CONV_DIM

SWA_HEAD_DIM = 64
SWA_Q_HEADS = 16
SWA_KV_HEADS = 4
SWA_BLOCK = 128
SWA_Q = SWA_Q_HEADS * SWA_HEAD_DIM
SWA_KV = SWA_KV_HEADS * SWA_HEAD_DIM
SWA_IN = SWA_Q + 2 * SWA_KV
ROT_DIM = 16
ROT_HALF = 8
ROPE_THETA = 500000.0

HGRN_HEADS = 8
HGRN_DK = 128
HGRN_CHUNK = 64
HGRN_SUB = 16
HGRN_TILE = 256

CONV_HALO = V7X_SUBLANES


def _cparams(*sem):
    return pltpu.CompilerParams(dimension_semantics=sem, vmem_limit_bytes=VMEM_LIMIT_BYTES)


def _sigmoid(x):
    return 1.0 / (1.0 + jnp.exp(-x))


def _silu(x):
    h = 0.5 * x
    return h + h * jnp.tanh(h)


def _rms_rows(x, g):
    ms = jnp.mean(x * x, axis=-1, keepdims=True)
    return x * lax.rsqrt(ms + EPS) * g


def _dot(a, b):
    return jnp.dot(a, b, preferred_element_type=F32)


def _dot_nt(a, b):
    return lax.dot_general(a, b, (((1,), (1,)), ((), ())), preferred_element_type=F32)


def _dot_exact_rhs(a, b01):
    a0 = a.astype(BF16)
    r1 = a - a0.astype(F32)
    a1 = r1.astype(BF16)
    a2 = (r1 - a1.astype(F32)).astype(BF16)
    return _dot(a0, b01) + _dot(a1, b01) + _dot(a2, b01)


def _dot_exact_lhs(a01, b):
    b0 = b.astype(BF16)
    r1 = b - b0.astype(F32)
    b1 = r1.astype(BF16)
    b2 = (r1 - b1.astype(F32)).astype(BF16)
    return _dot(a01, b0) + _dot(a01, b1) + _dot(a01, b2)


def _causal_conv(ext, w_ref, bias, taps):
    rows = ext.shape[0] - CONV_HALO
    acc = bias + w_ref[taps - 1:taps, :] * ext[:rows, :]
    for k in range(taps - 1):
        shift = taps - 1 - k
        acc = acc + w_ref[k:k + 1, :] * pltpu.roll(ext, shift, 0)[:rows, :]
    return acc


def _proj_kernel(x_ref, g_ref, w_ref, *rest, tm, halo, width, n_param, epilogue, tail):
    params = rest[:n_param]
    n_scr = 3 if halo else 2
    outs = rest[n_param:len(rest) - n_scr]
    hn_scr, u_scr = rest[len(rest) - n_scr:len(rest) - n_scr + 2]
    hn_scr[...] = _rms_rows(x_ref[...], g_ref[...]).astype(BF16)
    nc = w_ref.shape[1] // width
    if halo:
        prev_scr = rest[-1]
        first = pl.program_id(1) == 0

    def project(c):
        u_scr[c % 2, :tm] = _dot(hn_scr[...], w_ref[:, c * width:(c + 1) * width])
        if halo:
            u_scr[c % 2, tm:] = jnp.where(first, 0.0, prev_scr[c])
            prev_scr[c] = u_scr[c % 2, tm - halo:tm]

    project(0)
    for c in range(nc):
        if c + 1 < nc:
            project(c + 1)
        epilogue(c, u_scr.at[c % 2], params, outs)
    if tail is not None:
        tail(hn_scr, params, outs)


def _norm_proj(x2d, gain, w_bf16, params, out_widths, out_dtypes, epilogue, name, bsz, seq,
               tm=512, width=512, halo=0, tail=None):
    d = x2d.shape[1]
    nt = seq // tm
    row_map = lambda b, j: (b * nt + j, 0)
    whole = lambda a: pl.BlockSpec(a.shape, lambda b, j: (0,) * a.ndim)
    gain2 = gain.reshape(1, d)
    scratch = [pltpu.VMEM((tm, d), BF16), pltpu.VMEM((2, tm + halo, width), F32)]
    if halo:
        scratch.append(pltpu.VMEM((w_bf16.shape[1] // width, halo, width), F32))
    return pl.pallas_call(
        functools.partial(_proj_kernel, tm=tm, halo=halo, width=width, n_param=len(params),
                          epilogue=epilogue, tail=tail),
        grid=(bsz, nt),
        in_specs=[pl.BlockSpec((tm, d), row_map), whole(gain2), whole(w_bf16)] + [whole(p) for p in params],
        out_specs=[pl.BlockSpec((tm, w), row_map) for w in out_widths],
        out_shape=[jax.ShapeDtypeStruct((bsz * seq, w), dt) for w, dt in zip(out_widths, out_dtypes)],
        scratch_shapes=scratch,
        compiler_params=_cparams("parallel", "arbitrary" if halo else "parallel"),
        name=name,
    )(x2d, gain2, w_bf16, *params)


def _out_kernel(y_ref, w_ref, x_ref, o_ref):
    o_ref[...] = x_ref[...] + _dot(y_ref[...], w_ref[...])


def _out_proj(y_bf16, w_bf16, x2d, name, tm=512):
    t, k = y_bf16.shape
    d = w_bf16.shape[1]
    return pl.pallas_call(
        _out_kernel,
        grid=(t // tm,),
        in_specs=[pl.BlockSpec((tm, k), lambda i: (i, 0)),
                  pl.BlockSpec((k, d), lambda i: (0, 0)),
                  pl.BlockSpec((tm, d), lambda i: (i, 0))],
        out_specs=pl.BlockSpec((tm, d), lambda i: (i, 0)),
        out_shape=jax.ShapeDtypeStruct((t, d), F32),
        compiler_params=_cparams("parallel"),
        name=name,
    )(y_bf16, w_bf16, x2d)


LRU_WIDTH = 256


def _lru_epilogue(c, u_ref, params, outs):
    cw_ref, cb_ref, wg_ref, bg_ref, coef_ref = params
    gelu_ref, a_ref, b_ref = outs
    half = D_RNN // LRU_WIDTH
    if c < half:
        gate = u_ref[:u_ref.shape[0] - CONV_HALO, :]
        gelu_ref[:, c * LRU_WIDTH:(c + 1) * LRU_WIDTH] = (0.5 * gate * (
            1.0 + jnp.tanh(np.sqrt(2.0 / np.pi) * (gate + 0.044715 * (gate * gate * gate))))).astype(BF16)
        return
    cols = slice((c - half) * LRU_WIDTH, (c - half + 1) * LRU_WIDTH)
    xc = _causal_conv(u_ref[...], cw_ref.at[:, cols], cb_ref[:, cols], 4)
    for k in range(LRU_WIDTH // LRU_BLK):
        n = (c - half) * (LRU_WIDTH // LRU_BLK) + k
        lo, hi = n * LRU_BLK, (n + 1) * LRU_BLK
        xcn = xc[:, k * LRU_BLK:(k + 1) * LRU_BLK]
        g = _dot(xcn.astype(BF16), wg_ref[n]) + bg_ref[:, 2 * lo:2 * hi]
        r = _sigmoid(g[:, :LRU_BLK])
        i = _sigmoid(g[:, LRU_BLK:])
        log_a = coef_ref[:, lo:hi] * r
        a = jnp.exp(log_a)
        a_ref[:, lo:hi] = a
        b_ref[:, lo:hi] = jnp.sqrt(-jnp.tanh(log_a) * (a * a + 1.0)) * (i * xcn)


def _lru_scan_kernel(gelu_ref, a_ref, b_ref, w_ref, x_ref, o_ref, h_ref, hs_scr, *, tm):
    @pl.when(pl.program_id(1) == 0)
    def _():
        h_ref[...] = jnp.zeros_like(h_ref)

    row = lax.broadcasted_iota(jnp.int32, (V7X_SUBLANES, D_RNN), 0)

    def scan_rows(t, h):
        r0 = pl.multiple_of(t * V7X_SUBLANES, V7X_SUBLANES)
        a = a_ref[pl.ds(r0, V7X_SUBLANES), :]
        u = b_ref[pl.ds(r0, V7X_SUBLANES), :]
        for s in (1, 2, 4):
            keep = row >= s
            a_prev = jnp.where(keep, pltpu.roll(a, s, 0), 1.0)
            u_prev = jnp.where(keep, pltpu.roll(u, s, 0), 0.0)
            u = a * u_prev + u
            a = a * a_prev
        hs = a * h + u
        hs_scr[pl.ds(r0, V7X_SUBLANES), :] = hs
        return hs[V7X_SUBLANES - 1:, :]

    h_ref[...] = lax.fori_loop(0, tm // V7X_SUBLANES, scan_rows, h_ref[...], unroll=2)
    y = (gelu_ref[...].astype(F32) * hs_scr[...]).astype(BF16)
    o_ref[...] = x_ref[...] + _dot(y, w_ref[...])


def _lru_scan(gelu, a, b, w_out_bf16, x2d, bsz, seq, tm=256):
    nt = seq // tm
    row_map = lambda bb, j: (bb * nt + j, 0)
    spec = pl.BlockSpec((tm, D_RNN), row_map)
    xspec = pl.BlockSpec((tm, D_MODEL), row_map)
    return pl.pallas_call(
        functools.partial(_lru_scan_kernel, tm=tm),
        grid=(bsz, nt),
        in_specs=[spec, spec, spec, pl.BlockSpec((D_RNN, D_MODEL), lambda bb, j: (0, 0)), xspec],
        out_specs=xspec,
        out_shape=jax.ShapeDtypeStruct((bsz * seq, D_MODEL), F32),
        scratch_shapes=[pltpu.VMEM((1, D_RNN), F32),
                        pltpu.VMEM((tm, D_RNN), F32)],
        compiler_params=_cparams("parallel", "arbitrary"),
        name="lru_scan",
    )(gelu, a, b, w_out_bf16, x2d)


SSD_WIDTH = 512


def _softplus(x):
    return jnp.maximum(x, 0.0) + jnp.log1p(jnp.exp(-jnp.abs(x)))


def _ssd_epilogue(c, u_ref, params, outs):
    cw_ref, cb_ref = params[:2]
    zs_ref, xbc_ref = outs[:2]
    nz = D_INNER // SSD_WIDTH
    if c < nz:
        zs_ref[:, c * SSD_WIDTH:(c + 1) * SSD_WIDTH] = _silu(u_ref[:u_ref.shape[0] - CONV_HALO, :])
    else:
        cols = slice((c - nz) * SSD_WIDTH, (c - nz + 1) * SSD_WIDTH)
        xbc_ref[:, cols] = _silu(_causal_conv(u_ref[...], cw_ref.at[:, cols], cb_ref[:, cols], 4))


def _ssd_tail(hn_scr, params, outs):
    wdt_ref, dtb_ref = params[2:]
    outs[2][...] = _softplus(_dot(hn_scr[...], wdt_ref[...]) + dtb_ref[...])


def _ssd_core_kernel(zs_ref, xbc_ref, dt_ref, a_ref, dskip_ref, nw_ref, tri_ref, y_ref, st_ref):
    L = SSD_CHUNK
    hpg = SSD_HEADS // SSD_GROUPS

    @pl.when(pl.program_id(1) == 0)
    def _():
        st_ref[...] = jnp.zeros_like(st_ref)

    dt = dt_ref[...]
    cs = _dot_exact_lhs(tri_ref[...], dt * a_ref[...])
    cs_t = cs.T
    dt_t = dt.T

    ri = lax.broadcasted_iota(jnp.int32, (L, L), 0)
    ci = lax.broadcasted_iota(jnp.int32, (L, L), 1)
    causal = ri >= ci
    low_lanes = ci < SSD_HEAD_DIM

    bms = [xbc_ref[:, D_INNER + g * SSD_STATE:D_INNER + (g + 1) * SSD_STATE] for g in range(SSD_GROUPS)]
    cms = [xbc_ref[:, D_INNER + SSD_GN + g * SSD_STATE:D_INNER + SSD_GN + (g + 1) * SSD_STATE]
           for g in range(SSD_GROUPS)]
    cbs = [_dot_nt(cms[g].astype(BF16), bms[g].astype(BF16)) for g in range(SSD_GROUPS)]
    bm_ts = [bm.T for bm in bms]

    for g in range(SSD_GROUPS):
        heads = range(g * hpg, (g + 1) * hpg)
        lhs_y, lhs_s, decs = {}, {}, {}
        for h in heads:
            col = jnp.broadcast_to(cs[:, h:h + 1], (L, L))
            rowv = cs_t[h:h + 1, :]
            dt_row = dt_t[h:h + 1, :]
            last = cs_t[h:h + 1, L - 1:L]
            decay = jnp.exp(jnp.where(causal, col - rowv, -jnp.inf))
            w_diag = (cbs[g] * decay * dt_row).astype(BF16)
            c_off = (cms[g] * jnp.exp(col)).astype(BF16)
            lhs_y[h] = jnp.concatenate([w_diag, c_off], axis=1)
            lhs_s[h] = (bm_ts[g] * (jnp.exp(last - rowv) * dt_row)).astype(BF16)
            decs[h] = jnp.exp(last)
        rhs, xs_bf = {}, {}
        for pair in range(g * hpg // 2, (g + 1) * hpg // 2):
            lanes = slice(pair * V7X_LANES, (pair + 1) * V7X_LANES)
            xs_bf[pair] = xbc_ref[:, lanes].astype(BF16)
            rhs[pair] = jnp.concatenate([xs_bf[pair], st_ref[:, lanes].astype(BF16)], axis=0)
        ys = {h: _dot(lhs_y[h], rhs[h // 2]) for h in heads}
        sts = {h: _dot(lhs_s[h], xs_bf[h // 2]) for h in heads}
        y_pairs = []
        for pair in range(g * hpg // 2, (g + 1) * hpg // 2):
            lanes = slice(pair * V7X_LANES, (pair + 1) * V7X_LANES)
            h0, h1 = 2 * pair, 2 * pair + 1
            y_pairs.append(jnp.where(low_lanes, ys[h0], ys[h1]))
            dec = jnp.where(low_lanes[:1, :], decs[h0], decs[h1])
            st_ref[:, lanes] = st_ref[:, lanes] * dec + jnp.where(low_lanes, sts[h0], sts[h1])
        glanes = slice(g * 512, (g + 1) * 512)
        y = jnp.concatenate(y_pairs, axis=1) + xbc_ref[:, glanes] * dskip_ref[:, glanes]
        y = y * zs_ref[:, glanes]
        ms = jnp.mean(y * y, axis=-1, keepdims=True)
        y_ref[:, glanes] = (y * lax.rsqrt(ms + EPS) * nw_ref[:, glanes]).astype(BF16)


def _ssd_core(zs, xbc, dt, a_pad, d_skip_exp, norm_w, bsz, seq):
    L = SSD_CHUNK
    nt = seq // L
    row_map = lambda b, j: (b * nt + j, 0)
    const2 = lambda b, j: (0, 0)
    tri = jnp.tril(jnp.ones((L, L), F32)).astype(BF16)
    return pl.pallas_call(
        _ssd_core_kernel,
        grid=(bsz, nt),
        in_specs=[pl.BlockSpec((L, D_INNER), row_map),
                  pl.BlockSpec((L, SSD_CONV_DIM), row_map),
                  pl.BlockSpec((L, V7X_LANES), row_map),
                  pl.BlockSpec((1, V7X_LANES), const2),
                  pl.BlockSpec((1, D_INNER), const2),
                  pl.BlockSpec((1, D_INNER), const2),
                  pl.BlockSpec((L, L), const2)],
        out_specs=pl.BlockSpec((L, D_INNER), row_map),
        out_shape=jax.ShapeDtypeStruct((bsz * seq, D_INNER), BF16),
        scratch_shapes=[pltpu.VMEM((SSD_STATE, D_INNER), F32)],
        compiler_params=_cparams("parallel", "arbitrary"),
        name="ssd_core",
    )(zs, xbc, dt, a_pad, d_skip_exp, norm_w.reshape(1, D_INNER), tri)


SWA_WIDTH = 512


def _swa_epilogue(c, u_ref, params, outs):
    rope_ref, qn_ref, kn_ref, bd_ref = params
    out_ref = outs[0]
    tm = u_ref.shape[0]
    r0 = pl.multiple_of(pl.program_id(1) * tm, tm)
    rope = [rope_ref[k, pl.ds(r0, tm), :] for k in range(3)]
    for t in range(SWA_WIDTH // V7X_LANES):
        col = c * SWA_WIDTH + t * V7X_LANES
        x = u_ref[:, t * V7X_LANES:(t + 1) * V7X_LANES]
        if col < SWA_Q + SWA_KV:
            is_q = col < SWA_Q
            sq = x * x
            s0 = sq.astype(BF16)
            s1 = (sq - s0.astype(F32)).astype(BF16)
            ss = _dot(s0, bd_ref[...]) + _dot(s1, bd_ref[...])
            xn = x * lax.rsqrt(ss * (1.0 / SWA_HEAD_DIM) + EPS) * (qn_ref[...] if is_q else kn_ref[...])
            x = (xn * rope[0] + pltpu.roll(xn, V7X_LANES - ROT_HALF, 1) * rope[1]
                 + pltpu.roll(xn, ROT_HALF, 1) * rope[2])
            if is_q:
                x = x * (SWA_HEAD_DIM ** -0.5)
        out_ref[:, col:col + V7X_LANES] = x.astype(BF16)


SWA_TILE = 2 * SWA_BLOCK


def _swa_block(sink_ref, q_ref, k_cur, v_cur, k_prev, v_prev, first_key, y_ref):
    blk = SWA_BLOCK
    qi = lax.broadcasted_iota(jnp.int32, (blk, 2 * blk), 0)
    kj = lax.broadcasted_iota(jnp.int32, (blk, 2 * blk), 1)
    window = (kj > qi) & (kj <= qi + blk) & (kj >= first_key)
    lane = lax.broadcasted_iota(jnp.int32, (blk, V7X_LANES), 1)
    low = lane < SWA_HEAD_DIM
    n_pairs = SWA_Q // V7X_LANES
    half_masks = [jnp.where(lane[:1, :] < SWA_HEAD_DIM, 1.0, 0.0).astype(BF16),
                  jnp.where(lane[:1, :] < SWA_HEAD_DIM, 0.0, 1.0).astype(BF16)]
    kws, vws = [], []
    for kp in range(SWA_KV_HEADS // 2):
        lanes = slice(kp * V7X_LANES, (kp + 1) * V7X_LANES)
        kw = jnp.concatenate([k_prev[:, lanes], k_cur[:, lanes]], axis=0)
        kws.append([kw * hm for hm in half_masks])
        vws.append(jnp.concatenate([v_prev[:, lanes], v_cur[:, lanes]], axis=0))
    scores = []
    for pair in range(n_pairs):
        q = q_ref[:, pair * V7X_LANES:(pair + 1) * V7X_LANES]
        for half in range(2):
            scores.append(jnp.where(window, _dot_nt(q, kws[pair // 4][half]), -jnp.inf))
    probs, denoms = [], []
    for i, s in enumerate(scores):
        sink = sink_ref[i]
        m = jnp.maximum(jnp.max(s, axis=-1, keepdims=True), sink)
        p = jnp.exp(s - m)
        denoms.append(jnp.sum(p, axis=-1, keepdims=True) + jnp.exp(sink - m))
        probs.append(p.astype(BF16))
    outs = [_dot(probs[i], vws[i // 8]) / denoms[i] for i in range(2 * n_pairs)]
    for pair in range(n_pairs):
        y_ref[:, pair * V7X_LANES:(pair + 1) * V7X_LANES] = jnp.where(
            low, outs[2 * pair], outs[2 * pair + 1]).astype(BF16)


def _swa_core_kernel(sink_ref, p_ref, pkv_ref, w_ref, x_ref, o_ref, y_scr):
    blk = SWA_BLOCK
    for i in range(SWA_TILE // blk):
        rows = pl.ds(i * blk, blk)
        k_cur = p_ref[rows, SWA_Q:SWA_Q + SWA_KV]
        v_cur = p_ref[rows, SWA_Q + SWA_KV:]
        if i == 0:
            k_prev, v_prev = pkv_ref[:, :SWA_KV], pkv_ref[:, SWA_KV:]
            first_key = jnp.where(pl.program_id(1) == 0, blk, 0)
        else:
            prev_rows = pl.ds((i - 1) * blk, blk)
            k_prev = p_ref[prev_rows, SWA_Q:SWA_Q + SWA_KV]
            v_prev = p_ref[prev_rows, SWA_Q + SWA_KV:]
            first_key = 0
        _swa_block(sink_ref, p_ref.at[rows, :SWA_Q], k_cur, v_cur, k_prev, v_prev, first_key, y_scr.at[rows, :])
    o_ref[...] = x_ref[...] + _dot(y_scr[...], w_ref[...])


def _swa_core(sinks_perm, qkv, w_out_bf16, x2d, bsz, seq):
    tm = SWA_TILE
    nt = seq // tm
    bpt = tm // SWA_BLOCK
    row_map = lambda b, j: (b * nt + j, 0)
    prev_map = lambda b, j: (jnp.maximum((b * nt + j) * bpt - 1, 0), SWA_Q // (2 * SWA_KV))
    return pl.pallas_call(
        _swa_core_kernel,
        grid=(bsz, nt),
        in_specs=[pl.BlockSpec(memory_space=pltpu.SMEM),
                  pl.BlockSpec((tm, SWA_IN), row_map),
                  pl.BlockSpec((SWA_BLOCK, 2 * SWA_KV), prev_map),
                  pl.BlockSpec((SWA_Q, D_MODEL), lambda b, j: (0, 0)),
                  pl.BlockSpec((tm, D_MODEL), row_map)],
        out_specs=pl.BlockSpec((tm, D_MODEL), row_map),
        out_shape=jax.ShapeDtypeStruct((bsz * seq, D_MODEL), F32),
        scratch_shapes=[pltpu.VMEM((tm, SWA_Q), BF16)],
        compiler_params=_cparams("parallel", "parallel"),
        name="swa_core",
    )(sinks_perm, qkv, qkv, w_out_bf16, x2d)


def _swa_head_order():
    order = []
    for kp in range(SWA_KV_HEADS // 2):
        for i in range(4):
            order += [4 * (2 * kp) + i, 4 * (2 * kp + 1) + i]
    return np.asarray(order)


def _rope_tables(seq):
    inv_freq = ROPE_THETA ** (-np.arange(0, ROT_DIM, 2, dtype=np.float32) / ROT_DIM)
    ang = np.arange(seq, dtype=np.float32)[:, None] * inv_freq[None].astype(np.float32)
    cos, sin = np.cos(ang), np.sin(ang)
    ones = np.ones((seq, SWA_HEAD_DIM - ROT_DIM), np.float32)
    zeros = np.zeros((seq, SWA_HEAD_DIM - ROT_DIM), np.float32)
    z8 = np.zeros((seq, ROT_HALF), np.float32)
    c = np.concatenate([cos, cos, ones], axis=1)
    sa = np.concatenate([-sin, z8, zeros], axis=1)
    sb = np.concatenate([z8, sin, zeros], axis=1)
    tab = np.stack([c, sa, sb])
    return jnp.asarray(np.concatenate([tab, tab], axis=2), F32)


HGRN_WIDTH = 512
HGRN_FD = HGRN_HEADS * HGRN_DK
HGRN_F32_COLS = 2 * HGRN_FD
HGRN_BF16_COLS = 3 * HGRN_FD


def _hgrn_epilogue(c, u_ref, params, outs):
    lb_ref = params[0]
    f32_ref, bf16_ref = outs
    kind, k = divmod(c, HGRN_FD // HGRN_WIDTH)
    cols = slice(k * HGRN_WIDTH, (k + 1) * HGRN_WIDTH)
    u = u_ref[...]
    if kind == 0:
        bf16_ref[:, cols] = (u * (HGRN_DK ** -0.5)).astype(BF16)
    elif kind == 1:
        lb = lb_ref[:, cols]
        f = lb + (1.0 - lb) * _sigmoid(u)
        f32_ref[:, cols] = jnp.log(f)
        f32_ref[:, HGRN_FD + k * HGRN_WIDTH:HGRN_FD + (k + 1) * HGRN_WIDTH] = 1.0 - f
    elif kind == 2:
        bf16_ref[:, HGRN_FD + k * HGRN_WIDTH:HGRN_FD + (k + 1) * HGRN_WIDTH] = u.astype(BF16)
    else:
        bf16_ref[:, 2 * HGRN_FD + k * HGRN_WIDTH:2 * HGRN_FD + (k + 1) * HGRN_WIDTH] = _silu(u).astype(BF16)


def _hgrn_chunk_exact(h, r0, q, v, kk, cum, st_ref, o_scr, sub_row):
    L, C = HGRN_CHUNK, HGRN_SUB
    lanes = slice(h * HGRN_DK, (h + 1) * HGRN_DK)
    total = cum[L - 1:, :]
    st = st_ref[h]
    v_bf = v.astype(BF16)
    o = _dot_nt((q * jnp.exp(cum)).astype(BF16), st.astype(BF16))
    o_sub = []
    for i in range(L // C):
        qi = q[i * C:(i + 1) * C]
        cumi = cum[i * C:(i + 1) * C]
        oi = o[i * C:(i + 1) * C]
        if i > 0:
            ref = cum[i * C - 1:i * C]
            q_t = (qi * jnp.exp(cumi - ref)).astype(BF16)
            k_t = (kk[:i * C] * jnp.exp(ref - cum[:i * C])).astype(BF16)
            attn = _dot_nt(q_t, k_t)
            oi = oi + _dot(attn.astype(BF16), v_bf[:i * C])
        for s in range(C):
            r = i * C + s
            e = jnp.exp(jnp.where(sub_row >= s, cumi - cum[r:r + 1], -jnp.inf))
            w = jnp.sum(qi * kk[r:r + 1] * e, axis=-1, keepdims=True)
            oi = oi + w * v[r:r + 1]
        o_sub.append(oi)
    o_scr[pl.ds(r0, L), lanes] = jnp.concatenate(o_sub, axis=0)
    k_end = (kk * jnp.exp(total - cum)).astype(BF16)
    st_ref[h] = st * jnp.exp(total) + lax.dot_general(
        v_bf, k_end, (((0,), (0,)), ((), ())), preferred_element_type=F32)


def _hgrn_chunk_bounded(r0, qs, vs, kks, cums, st_ref, o_scr, causal):
    L = HGRN_CHUNK
    heads = range(HGRN_HEADS)
    mids = [c[L // 2 - 1:L // 2, :] for c in cums]
    totals = [c[L - 1:, :] for c in cums]
    sts = [st_ref[h] for h in heads]
    v_bf = [v.astype(BF16) for v in vs]
    qe = [qs[h] * jnp.exp(cums[h] - mids[h]) for h in heads]
    ke = [kks[h] * jnp.exp(mids[h] - cums[h]) for h in heads]
    attn = [jnp.where(causal, _dot_nt(qe[h].astype(BF16), ke[h].astype(BF16)), 0.0).astype(BF16) for h in heads]
    o_inter = [_dot_nt((qe[h] * jnp.exp(mids[h])).astype(BF16), sts[h].astype(BF16)) for h in heads]
    o_intra = [_dot(attn[h], v_bf[h]) for h in heads]
    upd = [lax.dot_general(v_bf[h], (ke[h] * jnp.exp(totals[h] - mids[h])).astype(BF16),
                           (((0,), (0,)), ((), ())), preferred_element_type=F32) for h in heads]
    for h in heads:
        o_scr[pl.ds(r0, L), h * HGRN_DK:(h + 1) * HGRN_DK] = o_intra[h] + o_inter[h]
        st_ref[h] = sts[h] * jnp.exp(totals[h]) + upd[h]


def _hgrn_core_kernel(bounded_ref, p_ref, pb_ref, nw_ref, tri_ref, w_ref, x_ref, out_ref, st_ref, o_scr, y_scr, *, tm):
    L, C = HGRN_CHUNK, HGRN_SUB
    fd = HGRN_FD

    @pl.when(pl.program_id(1) == 0)
    def _():
        st_ref[...] = jnp.zeros_like(st_ref)

    sub_row = lax.broadcasted_iota(jnp.int32, (C, HGRN_DK), 0)
    causal = lax.broadcasted_iota(jnp.int32, (L, L), 0) >= lax.broadcasted_iota(jnp.int32, (L, L), 1)
    tri = tri_ref[...]

    def operands(h, r0):
        lanes = slice(h * HGRN_DK, (h + 1) * HGRN_DK)
        q = pb_ref[pl.ds(r0, L), lanes].astype(F32)
        v = pb_ref[pl.ds(r0, L), fd + h * HGRN_DK:fd + (h + 1) * HGRN_DK].astype(F32)
        kk = p_ref[pl.ds(r0, L), fd + h * HGRN_DK:fd + (h + 1) * HGRN_DK]
        cum = _dot_exact_lhs(tri, p_ref[pl.ds(r0, L), lanes])
        return q, v, kk, cum

    def bounded_chunk(c):
        r0 = c * L
        cum_all = _dot_exact_lhs(tri, p_ref[pl.ds(r0, L), :fd])
        qs, vs, kks, cums = [], [], [], []
        for h in range(HGRN_HEADS):
            lanes = slice(h * HGRN_DK, (h + 1) * HGRN_DK)
            qs.append(pb_ref[pl.ds(r0, L), lanes].astype(F32))
            vs.append(pb_ref[pl.ds(r0, L), fd + h * HGRN_DK:fd + (h + 1) * HGRN_DK])
            kks.append(p_ref[pl.ds(r0, L), fd + h * HGRN_DK:fd + (h + 1) * HGRN_DK])
            cums.append(cum_all[:, lanes])
        _hgrn_chunk_bounded(r0, qs, vs, kks, cums, st_ref, o_scr, causal)

    def exact_chunk(c, carry):
        r0 = pl.multiple_of(c * L, L)
        for h in range(HGRN_HEADS):
            _hgrn_chunk_exact(h, r0, *operands(h, r0), st_ref, o_scr, sub_row)
        return carry

    @pl.when(bounded_ref[0] == 1)
    def _():
        for c in range(tm // L):
            bounded_chunk(c)

    @pl.when(bounded_ref[0] == 0)
    def _():
        lax.fori_loop(0, tm // L, exact_chunk, 0)

    for h in range(HGRN_HEADS):
        lanes = slice(h * HGRN_DK, (h + 1) * HGRN_DK)
        o = _rms_rows(o_scr[:, lanes], nw_ref[...])
        y_scr[:, lanes] = (o * pb_ref[:, 2 * fd + h * HGRN_DK:2 * fd + (h + 1) * HGRN_DK].astype(F32)).astype(BF16)
    out_ref[...] = x_ref[...] + _dot(y_scr[...], w_ref[...])


HGRN_MAX_EXPONENT = 80.0


def _hgrn_core(prep_f32, prep_bf16, lower_bound, norm_w, w_out_bf16, x2d, bsz, seq, tm=HGRN_TILE):
    nt = seq // tm
    fd = HGRN_FD
    row_map = lambda b, j: (b * nt + j, 0)
    const2 = lambda b, j: (0, 0)
    tri = jnp.tril(jnp.ones((HGRN_CHUNK, HGRN_CHUNK), F32)).astype(BF16)
    worst = -(HGRN_CHUNK // 2) * jnp.log(jnp.min(lower_bound))
    bounded = (worst <= HGRN_MAX_EXPONENT).astype(jnp.int32).reshape(1)
    return pl.pallas_call(
        functools.partial(_hgrn_core_kernel, tm=tm),
        grid=(bsz, nt),
        in_specs=[pl.BlockSpec(memory_space=pltpu.SMEM),
                  pl.BlockSpec((tm, HGRN_F32_COLS), row_map),
                  pl.BlockSpec((tm, HGRN_BF16_COLS), row_map),
                  pl.BlockSpec((1, HGRN_DK), const2),
                  pl.BlockSpec((HGRN_CHUNK, HGRN_CHUNK), const2),
                  pl.BlockSpec((fd, D_MODEL), const2),
                  pl.BlockSpec((tm, D_MODEL), row_map)],
        out_specs=pl.BlockSpec((tm, D_MODEL), row_map),
        out_shape=jax.ShapeDtypeStruct((bsz * seq, D_MODEL), F32),
        scratch_shapes=[pltpu.VMEM((HGRN_HEADS, HGRN_DK, HGRN_DK), F32),
                        pltpu.VMEM((tm, fd), F32),
                        pltpu.VMEM((tm, fd), BF16)],
        compiler_params=_cparams("parallel", "arbitrary"),
        name="hgrn_core",
    )(bounded, prep_f32, prep_bf16, norm_w.reshape(1, HGRN_DK), tri, w_out_bf16, x2d)


def _ffn_kernel(x_ref, g_ref, wu_ref, cw_ref, cb_ref, wd_ref, o_ref, hn_scr, u_scr, act_scr, prev_scr, *, tm):
    fc = FFN_CHUNK
    first = pl.program_id(1) == 0
    hn_scr[...] = _rms_rows(x_ref[...], g_ref[...]).astype(BF16)

    def up(c):
        for half in range(2):
            cols = slice(half * D_FF + c * fc, half * D_FF + (c + 1) * fc)
            u_scr[c % 2, half, :tm] = _dot(hn_scr[...], wu_ref[:, cols])
            u_scr[c % 2, half, tm:] = jnp.where(first, 0.0, prev_scr[c, half])
            prev_scr[c, half] = u_scr[c % 2, half, tm - CONV_HALO:tm]

    def gate(c):
        u = []
        for half in range(2):
            cols = slice(half * D_FF + c * fc, half * D_FF + (c + 1) * fc)
            u.append(_causal_conv(u_scr[c % 2, half], cw_ref.at[:, cols], cb_ref[:, cols], 3))
        act_scr[:, c * fc:(c + 1) * fc] = (_silu(u[1]) * u[0]).astype(BF16)

    up(0)
    for c in range(N_FFN_CHUNKS):
        if c + 1 < N_FFN_CHUNKS:
            up(c + 1)
        gate(c)
    o_ref[...] = x_ref[...] + _dot(act_scr[...], wd_ref[...])


def _ffn(x2d, gain, w_up_bf16, conv_w, conv_b, w_down_bf16, layer, bsz, seq, tm=512):
    nt = seq // tm
    fc = FFN_CHUNK
    row_map = lambda b, j: (b * nt + j, 0)
    const2 = lambda b, j: (0, 0)
    of_layer = lambda b, j: (layer, 0, 0)
    return pl.pallas_call(
        functools.partial(_ffn_kernel, tm=tm),
        grid=(bsz, nt),
        in_specs=[pl.BlockSpec((tm, D_MODEL), row_map),
                  pl.BlockSpec((1, D_MODEL), const2),
                  pl.BlockSpec((None, D_MODEL, 2 * D_FF), of_layer),
                  pl.BlockSpec((None, 3, 2 * D_FF), of_layer),
                  pl.BlockSpec((None, 1, 2 * D_FF), of_layer),
                  pl.BlockSpec((None, D_FF, D_MODEL), of_layer)],
        out_specs=pl.BlockSpec((tm, D_MODEL), row_map),
        out_shape=jax.ShapeDtypeStruct((bsz * seq, D_MODEL), F32),
        scratch_shapes=[pltpu.VMEM((tm, D_MODEL), BF16),
                        pltpu.VMEM((2, 2, tm + CONV_HALO, fc), F32),
                        pltpu.VMEM((tm, D_FF), BF16),
                        pltpu.VMEM((N_FFN_CHUNKS, 2, CONV_HALO, fc), F32)],
        compiler_params=_cparams("parallel", "arbitrary"),
        name="conv_ffn",
    )(x2d, gain.reshape(1, D_MODEL), w_up_bf16, conv_w, conv_b.reshape(-1, 1, 2 * D_FF), w_down_bf16)


def _lru_mixer(x2d, gain, w_in, conv_w, conv_b, w_gate, b_gate, lam, w_out, bsz, seq):
    coef = (-LRU_C * jax.nn.softplus(-lam.astype(F32))).reshape(1, D_RNN)
    params = [conv_w, conv_b.reshape(1, D_RNN), w_gate.astype(BF16), b_gate.reshape(1, 2 * D_RNN), coef]
    gelu, a, b = _norm_proj(x2d, gain, w_in.astype(BF16), params, [D_RNN] * 3, [BF16, F32, F32], _lru_epilogue,
                            "lru_in", bsz, seq, width=LRU_WIDTH, halo=CONV_HALO)
    return _lru_scan(gelu, a, b, w_out.astype(BF16), x2d, bsz, seq)


def _ssd_mixer(x2d, gain, w_in, conv_w, conv_b, dt_bias, a_log, d_skip, norm_w, w_out, bsz, seq):
    pad = V7X_LANES - SSD_HEADS
    w_dt = jnp.pad(w_in[:, SSD_MAIN:], ((0, 0), (0, pad))).astype(BF16)
    dt_bias_pad = jnp.pad(dt_bias.astype(F32), (0, pad)).reshape(1, V7X_LANES)
    params = [conv_w, conv_b.reshape(1, SSD_CONV_DIM), w_dt, dt_bias_pad]
    zs, xbc, dt = _norm_proj(x2d, gain, w_in[:, :SSD_MAIN].astype(BF16), params,
                             [D_INNER, SSD_CONV_DIM, V7X_LANES], [F32] * 3, _ssd_epilogue, "ssd_in", bsz, seq,
                             width=SSD_WIDTH, halo=CONV_HALO, tail=_ssd_tail)
    a_pad = jnp.pad(-jnp.exp(a_log.astype(F32)), (0, pad)).reshape(1, V7X_LANES)
    d_exp = jnp.repeat(d_skip.astype(F32), SSD_HEAD_DIM).reshape(1, D_INNER)
    y = _ssd_core(zs, xbc, dt, a_pad, d_exp, norm_w, bsz, seq)
    return _out_proj(y, w_out.astype(BF16), x2d, "ssd_out")


def _swa_mixer(x2d, gain, w_in, q_norm, k_norm, sinks, w_out, bsz, seq):
    order = _swa_head_order()
    cols = (order[:, None] * SWA_HEAD_DIM + np.arange(SWA_HEAD_DIM)[None]).reshape(-1)
    w_in_perm = jnp.concatenate([w_in[:, cols], w_in[:, SWA_Q:]], axis=1).astype(BF16)
    q_gain = jnp.tile(q_norm.astype(F32), 2).reshape(1, V7X_LANES)
    k_gain = jnp.tile(k_norm.astype(F32), 2).reshape(1, V7X_LANES)
    half = np.arange(V7X_LANES) // SWA_HEAD_DIM
    bd = jnp.asarray(half[:, None] == half[None, :], BF16)
    qkv, = _norm_proj(x2d, gain, w_in_perm, [_rope_tables(seq), q_gain, k_gain, bd], [SWA_IN], [BF16],
                      _swa_epilogue, "swa_in", bsz, seq, width=SWA_WIDTH)
    return _swa_core(sinks.astype(F32)[order], qkv, w_out[cols, :].astype(BF16), x2d, bsz, seq)


def _hgrn_mixer(x2d, gain, w_in, norm_w, w_out, lower_bound, bsz, seq):
    prep_f32, prep_bf16 = _norm_proj(x2d, gain, w_in.astype(BF16), [lower_bound.reshape(1, HGRN_FD)],
                                     [HGRN_F32_COLS, HGRN_BF16_COLS], [F32, BF16], _hgrn_epilogue, "hgrn_in",
                                     bsz, seq, width=HGRN_WIDTH)
    return _hgrn_core(prep_f32, prep_bf16, lower_bound, norm_w, w_out.astype(BF16), x2d, bsz, seq)


def kernel(x, mix_norm, ffn_norm, ffn_w_up, ffn_conv_w, ffn_conv_b, ffn_w_down, lru_w_in, lru_conv_w, lru_conv_b, lru_w_gate, lru_b_gate, lru_lambda, lru_w_out, ssd_w_in, ssd_conv_w, ssd_conv_b, ssd_dt_bias, ssd_a_log, ssd_d, ssd_norm, ssd_w_out, swa_w_in, swa_q_norm, swa_k_norm, swa_sinks, swa_w_out, hgrn_w_in, hgrn_norm, hgrn_w_out, hgrn_lower_bounds):
    bsz, seq, d = x.shape
    depth = mix_norm.shape[0]
    lbs = jnp.cumsum(jax.nn.softmax(hgrn_lower_bounds.astype(F32), axis=0), axis=0)
    lbs = lbs - lbs[0]
    ffn_w_up_bf16 = ffn_w_up.astype(BF16)
    ffn_w_down_bf16 = ffn_w_down.astype(BF16)
    h = x.reshape(bsz * seq, d)
    for i in range(depth):
        kind, j = i % 4, i // 4
        if kind == 0:
            h = _lru_mixer(h, mix_norm[i], lru_w_in[j], lru_conv_w[j], lru_conv_b[j], lru_w_gate[j],
                           lru_b_gate[j], lru_lambda[j], lru_w_out[j], bsz, seq)
        elif kind == 1:
            h = _ssd_mixer(h, mix_norm[i], ssd_w_in[j], ssd_conv_w[j], ssd_conv_b[j], ssd_dt_bias[j],
                           ssd_a_log[j], ssd_d[j], ssd_norm[j], ssd_w_out[j], bsz, seq)
        elif kind == 2:
            h = _swa_mixer(h, mix_norm[i], swa_w_in[j], swa_q_norm[j], swa_k_norm[j], swa_sinks[j],
                           swa_w_out[j], bsz, seq)
        else:
            h = _hgrn_mixer(h, mix_norm[i], hgrn_w_in[j], hgrn_norm[j], hgrn_w_out[j], lbs[i], bsz, seq)
        h = _ffn(h, ffn_norm[i], ffn_w_up_bf16, ffn_conv_w, ffn_conv_b, ffn_w_down_bf16, i, bsz, seq)
    return h.reshape(bsz, seq, d)
```

```python
import functools

import jax
import jax.numpy as jnp
import numpy as np
from jax import lax
from jax.experimental import pallas as pl
from jax.experimental.pallas import tpu as pltpu

F32 = jnp.float32
BF16 = jnp.bfloat16
EPS = 1e-6

V7X_LANES = 128
V7X_SUBLANES = 8
V7X_VMEM_BYTES = 64 * 1024 * 1024
VMEM_LIMIT_BYTES = V7X_VMEM_BYTES - 8 * 1024 * 1024

D_MODEL = 1024
D_FF = 2816
FFN_CHUNK = 256
N_FFN_CHUNKS = D_FF // FFN_CHUNK

D_RNN = 1280
LRU_BLOCKS = 10
LRU_BLK = 128
LRU_C = 8.0

D_INNER = 2048
SSD_HEADS = 32
SSD_HEAD_DIM = 64
SSD_GROUPS = 4
SSD_STATE = 128
SSD_CHUNK = 128
SSD_GN = SSD_GROUPS * SSD_STATE
SSD_CONV_DIM = D_INNER + 2 * SSD_GN
SSD_MAIN = D_INNER + SSD_CONV_DIM

SWA_HEAD_DIM = 64
SWA_Q_HEADS = 16
SWA_KV_HEADS = 4
SWA_BLOCK = 128
SWA_Q = SWA_Q_HEADS * SWA_HEAD_DIM
SWA_KV = SWA_KV_HEADS * SWA_HEAD_DIM
SWA_IN = SWA_Q + 2 * SWA_KV
ROT_DIM = 16
ROT_HALF = 8
ROPE_THETA = 500000.0

HGRN_HEADS = 8
HGRN_DK = 128
HGRN_CHUNK = 64
HGRN_SUB = 16
HGRN_TILE = 512

CONV_HALO = V7X_SUBLANES


def _cparams(*sem):
    return pltpu.CompilerParams(dimension_semantics=sem, vmem_limit_bytes=VMEM_LIMIT_BYTES)


def _sigmoid(x):
    return 1.0 / (1.0 + jnp.exp(-x))


def _silu(x):
    h = 0.5 * x
    return h + h * jnp.tanh(h)


def _rms_rows(x, g):
    ms = jnp.mean(x * x, axis=-1, keepdims=True)
    return x * lax.rsqrt(ms + EPS) * g


def _dot(a, b):
    return jnp.dot(a, b, preferred_element_type=F32)


def _dot_nt(a, b):
    return lax.dot_general(a, b, (((1,), (1,)), ((), ())), preferred_element_type=F32)


def _dot_exact_rhs(a, b01):
    a0 = a.astype(BF16)
    r1 = a - a0.astype(F32)
    a1 = r1.astype(BF16)
    a2 = (r1 - a1.astype(F32)).astype(BF16)
    return _dot(a0, b01) + _dot(a1, b01) + _dot(a2, b01)


def _dot_exact_lhs(a01, b):
    b0 = b.astype(BF16)
    r1 = b - b0.astype(F32)
    b1 = r1.astype(BF16)
    b2 = (r1 - b1.astype(F32)).astype(BF16)
    return _dot(a01, b0) + _dot(a01, b1) + _dot(a01, b2)


def _causal_conv(ext, w_ref, bias, taps):
    rows = ext.shape[0] - CONV_HALO
    acc = bias + w_ref[taps - 1:taps, :] * ext[:rows, :]
    for k in range(taps - 1):
        shift = taps - 1 - k
        acc = acc + w_ref[k:k + 1, :] * pltpu.roll(ext, shift, 0)[:rows, :]
    return acc


def _proj_kernel(x_ref, g_ref, w_ref, *rest, tm, halo, width, n_param, epilogue, tail):
    params = rest[:n_param]
    n_scr = 3 if halo else 2
    outs = rest[n_param:len(rest) - n_scr]
    hn_scr, u_scr = rest[len(rest) - n_scr:len(rest) - n_scr + 2]
    hn_scr[...] = _rms_rows(x_ref[...], g_ref[...]).astype(BF16)
    nc = w_ref.shape[1] // width
    if halo:
        prev_scr = rest[-1]
        first = pl.program_id(1) == 0

    def project(c):
        u_scr[c % 2, :tm] = _dot(hn_scr[...], w_ref[:, c * width:(c + 1) * width])
        if halo:
            u_scr[c % 2, tm:] = jnp.where(first, 0.0, prev_scr[c])
            prev_scr[c] = u_scr[c % 2, tm - halo:tm]

    project(0)
    for c in range(nc):
        if c + 1 < nc:
            project(c + 1)
        epilogue(c, u_scr.at[c % 2], params, outs)
    if tail is not None:
        tail(hn_scr, params, outs)


def _norm_proj(x2d, gain, w_bf16, params, out_widths, out_dtypes, epilogue, name, bsz, seq,
               tm=512, width=512, halo=0, tail=None):
    d = x2d.shape[1]
    nt = seq // tm
    row_map = lambda b, j: (b * nt + j, 0)
    whole = lambda a: pl.BlockSpec(a.shape, lambda b, j: (0,) * a.ndim)
    gain2 = gain.reshape(1, d)
    scratch = [pltpu.VMEM((tm, d), BF16), pltpu.VMEM((2, tm + halo, width), F32)]
    if halo:
        scratch.append(pltpu.VMEM((w_bf16.shape[1] // width, halo, width), F32))
    return pl.pallas_call(
        functools.partial(_proj_kernel, tm=tm, halo=halo, width=width, n_param=len(params),
                          epilogue=epilogue, tail=tail),
        grid=(bsz, nt),
        in_specs=[pl.BlockSpec((tm, d), row_map), whole(gain2), whole(w_bf16)] + [whole(p) for p in params],
        out_specs=[pl.BlockSpec((tm, w), row_map) for w in out_widths],
        out_shape=[jax.ShapeDtypeStruct((bsz * seq, w), dt) for w, dt in zip(out_widths, out_dtypes)],
        scratch_shapes=scratch,
        compiler_params=_cparams("parallel", "arbitrary" if halo else "parallel"),
        name=name,
    )(x2d, gain2, w_bf16, *params)


LRU_WIDTH = 256


def _lru_epilogue(c, u_ref, params, outs):
    cw_ref, cb_ref, wg_ref, bg_ref, coef_ref = params
    gelu_ref, a_ref, b_ref = outs
    half = D_RNN // LRU_WIDTH
    if c < half:
        gate = u_ref[:u_ref.shape[0] - CONV_HALO, :]
        gelu_ref[:, c * LRU_WIDTH:(c + 1) * LRU_WIDTH] = (0.5 * gate * (
            1.0 + jnp.tanh(np.sqrt(2.0 / np.pi) * (gate + 0.044715 * (gate * gate * gate))))).astype(BF16)
        return
    cols = slice((c - half) * LRU_WIDTH, (c - half + 1) * LRU_WIDTH)
    xc = _causal_conv(u_ref[...], cw_ref.at[:, cols], cb_ref[:, cols], 4)
    for k in range(LRU_WIDTH // LRU_BLK):
        n = (c - half) * (LRU_WIDTH // LRU_BLK) + k
        lo, hi = n * LRU_BLK, (n + 1) * LRU_BLK
        xcn = xc[:, k * LRU_BLK:(k + 1) * LRU_BLK]
        g = _dot(xcn.astype(BF16), wg_ref[n]) + bg_ref[:, 2 * lo:2 * hi]
        r = _sigmoid(g[:, :LRU_BLK])
        i = _sigmoid(g[:, LRU_BLK:])
        log_a = coef_ref[:, lo:hi] * r
        a = jnp.exp(log_a)
        a_ref[:, lo:hi] = a
        b_ref[:, lo:hi] = jnp.sqrt(-jnp.tanh(log_a) * (a * a + 1.0)) * (i * xcn)


def _lru_scan_kernel(gelu_ref, a_ref, b_ref, w_ref, x_ref, o_ref, h_ref, hs_scr, *, tm):
    @pl.when(pl.program_id(1) == 0)
    def _():
        h_ref[...] = jnp.zeros_like(h_ref)

    row = lax.broadcasted_iota(jnp.int32, (V7X_SUBLANES, D_RNN), 0)

    def scan_rows(t, h):
        r0 = pl.multiple_of(t * V7X_SUBLANES, V7X_SUBLANES)
        a = a_ref[pl.ds(r0, V7X_SUBLANES), :]
        u = b_ref[pl.ds(r0, V7X_SUBLANES), :]
        for s in (1, 2, 4):
            keep = row >= s
            a_prev = jnp.where(keep, pltpu.roll(a, s, 0), 1.0)
            u_prev = jnp.where(keep, pltpu.roll(u, s, 0), 0.0)
            u = a * u_prev + u
            a = a * a_prev
        hs = a * h + u
        hs_scr[pl.ds(r0, V7X_SUBLANES), :] = hs
        return hs[V7X_SUBLANES - 1:, :]

    h_ref[...] = lax.fori_loop(0, tm // V7X_SUBLANES, scan_rows, h_ref[...], unroll=2)
    y = (gelu_ref[...].astype(F32) * hs_scr[...]).astype(BF16)
    o_ref[...] = x_ref[...] + _dot(y, w_ref[...])


def _lru_scan(gelu, a, b, w_out_bf16, x2d, bsz, seq, tm=512):
    nt = seq // tm
    row_map = lambda bb, j: (bb * nt + j, 0)
    spec = pl.BlockSpec((tm, D_RNN), row_map)
    xspec = pl.BlockSpec((tm, D_MODEL), row_map)
    return pl.pallas_call(
        functools.partial(_lru_scan_kernel, tm=tm),
        grid=(bsz, nt),
        in_specs=[spec, spec, spec, pl.BlockSpec((D_RNN, D_MODEL), lambda bb, j: (0, 0)), xspec],
        out_specs=xspec,
        out_shape=jax.ShapeDtypeStruct((bsz * seq, D_MODEL), F32),
        scratch_shapes=[pltpu.VMEM((1, D_RNN), F32),
                        pltpu.VMEM((tm, D_RNN), F32)],
        compiler_params=_cparams("parallel", "arbitrary"),
        name="lru_scan",
    )(gelu, a, b, w_out_bf16, x2d)


SSD_WIDTH = 512


def _softplus(x):
    return jnp.maximum(x, 0.0) + jnp.log1p(jnp.exp(-jnp.abs(x)))


def _ssd_epilogue(c, u_ref, params, outs):
    cw_ref, cb_ref = params[:2]
    zs_ref, xbc_ref = outs[:2]
    nz = D_INNER // SSD_WIDTH
    if c < nz:
        zs_ref[:, c * SSD_WIDTH:(c + 1) * SSD_WIDTH] = _silu(u_ref[:u_ref.shape[0] - CONV_HALO, :])
    else:
        cols = slice((c - nz) * SSD_WIDTH, (c - nz + 1) * SSD_WIDTH)
        xbc_ref[:, cols] = _silu(_causal_conv(u_ref[...], cw_ref.at[:, cols], cb_ref[:, cols], 4))


def _ssd_tail(hn_scr, params, outs):
    wdt_ref, dtb_ref = params[2:]
    outs[2][...] = _softplus(_dot(hn_scr[...], wdt_ref[...]) + dtb_ref[...])


SSD_TILE = 2 * SSD_CHUNK


def _ssd_chunk(zs_ref, xbc_ref, dt_ref, a_ref, dskip_ref, nw_ref, tri_ref, st_ref, y_ref):
    L = SSD_CHUNK
    hpg = SSD_HEADS // SSD_GROUPS

    dt = dt_ref[...]
    cs = _dot_exact_lhs(tri_ref[...], dt * a_ref[...])
    cs_t = cs.T
    dt_t = dt.T

    ri = lax.broadcasted_iota(jnp.int32, (L, L), 0)
    ci = lax.broadcasted_iota(jnp.int32, (L, L), 1)
    causal = ri >= ci
    low_lanes = ci < SSD_HEAD_DIM

    bms = [xbc_ref[:, D_INNER + g * SSD_STATE:D_INNER + (g + 1) * SSD_STATE] for g in range(SSD_GROUPS)]
    cms = [xbc_ref[:, D_INNER + SSD_GN + g * SSD_STATE:D_INNER + SSD_GN + (g + 1) * SSD_STATE]
           for g in range(SSD_GROUPS)]
    cbs = [_dot_nt(cms[g].astype(BF16), bms[g].astype(BF16)) for g in range(SSD_GROUPS)]
    bm_ts = [bm.T for bm in bms]

    for g in range(SSD_GROUPS):
        heads = range(g * hpg, (g + 1) * hpg)
        lhs_y, lhs_s, decs = {}, {}, {}
        for h in heads:
            col = jnp.broadcast_to(cs[:, h:h + 1], (L, L))
            rowv = cs_t[h:h + 1, :]
            dt_row = dt_t[h:h + 1, :]
            last = cs_t[h:h + 1, L - 1:L]
            decay = jnp.exp(jnp.where(causal, col - rowv, -jnp.inf))
            w_diag = (cbs[g] * decay * dt_row).astype(BF16)
            c_off = (cms[g] * jnp.exp(col)).astype(BF16)
            lhs_y[h] = jnp.concatenate([w_diag, c_off], axis=1)
            lhs_s[h] = (bm_ts[g] * (jnp.exp(last - rowv) * dt_row)).astype(BF16)
            decs[h] = jnp.exp(last)
        rhs, xs_bf = {}, {}
        for pair in range(g * hpg // 2, (g + 1) * hpg // 2):
            lanes = slice(pair * V7X_LANES, (pair + 1) * V7X_LANES)
            xs_bf[pair] = xbc_ref[:, lanes].astype(BF16)
            rhs[pair] = jnp.concatenate([xs_bf[pair], st_ref[:, lanes].astype(BF16)], axis=0)
        ys = {h: _dot(lhs_y[h], rhs[h // 2]) for h in heads}
        sts = {h: _dot(lhs_s[h], xs_bf[h // 2]) for h in heads}
        y_pairs = []
        for pair in range(g * hpg // 2, (g + 1) * hpg // 2):
            lanes = slice(pair * V7X_LANES, (pair + 1) * V7X_LANES)
            h0, h1 = 2 * pair, 2 * pair + 1
            y_pairs.append(jnp.where(low_lanes, ys[h0], ys[h1]))
            dec = jnp.where(low_lanes[:1, :], decs[h0], decs[h1])
            st_ref[:, lanes] = st_ref[:, lanes] * dec + jnp.where(low_lanes, sts[h0], sts[h1])
        glanes = slice(g * 512, (g + 1) * 512)
        y = jnp.concatenate(y_pairs, axis=1) + xbc_ref[:, glanes] * dskip_ref[:, glanes]
        y = y * zs_ref[:, glanes]
        ms = jnp.mean(y * y, axis=-1, keepdims=True)
        y_ref[:, glanes] = (y * lax.rsqrt(ms + EPS) * nw_ref[:, glanes]).astype(BF16)


def _ssd_core_kernel(zs_ref, xbc_ref, dt_ref, a_ref, dskip_ref, nw_ref, tri_ref, w_ref, x_ref, o_ref, st_ref, y_scr):
    @pl.when(pl.program_id(1) == 0)
    def _():
        st_ref[...] = jnp.zeros_like(st_ref)

    for i in range(SSD_TILE // SSD_CHUNK):
        rows = pl.ds(i * SSD_CHUNK, SSD_CHUNK)
        _ssd_chunk(zs_ref.at[rows, :], xbc_ref.at[rows, :], dt_ref.at[rows, :], a_ref, dskip_ref, nw_ref, tri_ref,
                   st_ref, y_scr.at[rows, :])
    o_ref[...] = x_ref[...] + _dot(y_scr[...], w_ref[...])


def _ssd_core(zs, xbc, dt, a_pad, d_skip_exp, norm_w, w_out_bf16, x2d, bsz, seq):
    L = SSD_CHUNK
    tm = SSD_TILE
    nt = seq // tm
    row_map = lambda b, j: (b * nt + j, 0)
    const2 = lambda b, j: (0, 0)
    tri = jnp.tril(jnp.ones((L, L), F32)).astype(BF16)
    return pl.pallas_call(
        _ssd_core_kernel,
        grid=(bsz, nt),
        in_specs=[pl.BlockSpec((tm, D_INNER), row_map),
                  pl.BlockSpec((tm, SSD_CONV_DIM), row_map),
                  pl.BlockSpec((tm, V7X_LANES), row_map),
                  pl.BlockSpec((1, V7X_LANES), const2),
                  pl.BlockSpec((1, D_INNER), const2),
                  pl.BlockSpec((1, D_INNER), const2),
                  pl.BlockSpec((L, L), const2),
                  pl.BlockSpec((D_INNER, D_MODEL), const2),
                  pl.BlockSpec((tm, D_MODEL), row_map)],
        out_specs=pl.BlockSpec((tm, D_MODEL), row_map),
        out_shape=jax.ShapeDtypeStruct((bsz * seq, D_MODEL), F32),
        scratch_shapes=[pltpu.VMEM((SSD_STATE, D_INNER), F32),
                        pltpu.VMEM((tm, D_INNER), BF16)],
        compiler_params=_cparams("parallel", "arbitrary"),
        name="ssd_core",
    )(zs, xbc, dt, a_pad, d_skip_exp, norm_w.reshape(1, D_INNER), tri, w_out_bf16, x2d)


SWA_WIDTH = 512


def _swa_epilogue(c, u_ref, params, outs):
    rope_ref, qn_ref, kn_ref, bd_ref = params
    out_ref = outs[0]
    tm = u_ref.shape[0]
    r0 = pl.multiple_of(pl.program_id(1) * tm, tm)
    rope = [rope_ref[k, pl.ds(r0, tm), :] for k in range(3)]
    for t in range(SWA_WIDTH // V7X_LANES):
        col = c * SWA_WIDTH + t * V7X_LANES
        x = u_ref[:, t * V7X_LANES:(t + 1) * V7X_LANES]
        if col < SWA_Q + SWA_KV:
            is_q = col < SWA_Q
            sq = x * x
            s0 = sq.astype(BF16)
            s1 = (sq - s0.astype(F32)).astype(BF16)
            ss = _dot(s0, bd_ref[...]) + _dot(s1, bd_ref[...])
            xn = x * lax.rsqrt(ss * (1.0 / SWA_HEAD_DIM) + EPS) * (qn_ref[...] if is_q else kn_ref[...])
            x = (xn * rope[0] + pltpu.roll(xn, V7X_LANES - ROT_HALF, 1) * rope[1]
                 + pltpu.roll(xn, ROT_HALF, 1) * rope[2])
            if is_q:
                x = x * (SWA_HEAD_DIM ** -0.5)
        out_ref[:, col:col + V7X_LANES] = x.astype(BF16)


SWA_TILE = 2 * SWA_BLOCK


def _swa_block(sink_ref, q_ref, k_cur, v_cur, k_prev, v_prev, first_key, y_ref):
    blk = SWA_BLOCK
    qi = lax.broadcasted_iota(jnp.int32, (blk, 2 * blk), 0)
    kj = lax.broadcasted_iota(jnp.int32, (blk, 2 * blk), 1)
    window = (kj > qi) & (kj <= qi + blk) & (kj >= first_key)
    lane = lax.broadcasted_iota(jnp.int32, (blk, V7X_LANES), 1)
    low = lane < SWA_HEAD_DIM
    n_pairs = SWA_Q // V7X_LANES
    half_masks = [jnp.where(lane[:1, :] < SWA_HEAD_DIM, 1.0, 0.0).astype(BF16),
                  jnp.where(lane[:1, :] < SWA_HEAD_DIM, 0.0, 1.0).astype(BF16)]
    kws, vws = [], []
    for kp in range(SWA_KV_HEADS // 2):
        lanes = slice(kp * V7X_LANES, (kp + 1) * V7X_LANES)
        kw = jnp.concatenate([k_prev[:, lanes], k_cur[:, lanes]], axis=0)
        kws.append([kw * hm for hm in half_masks])
        vws.append(jnp.concatenate([v_prev[:, lanes], v_cur[:, lanes]], axis=0))
    scores = []
    for pair in range(n_pairs):
        q = q_ref[:, pair * V7X_LANES:(pair + 1) * V7X_LANES]
        for half in range(2):
            scores.append(jnp.where(window, _dot_nt(q, kws[pair // 4][half]), -jnp.inf))
    probs, denoms = [], []
    for i, s in enumerate(scores):
        sink = sink_ref[i]
        m = jnp.maximum(jnp.max(s, axis=-1, keepdims=True), sink)
        p = jnp.exp(s - m)
        denoms.append(jnp.sum(p, axis=-1, keepdims=True) + jnp.exp(sink - m))
        probs.append(p.astype(BF16))
    outs = [_dot(probs[i], vws[i // 8]) / denoms[i] for i in range(2 * n_pairs)]
    for pair in range(n_pairs):
        y_ref[:, pair * V7X_LANES:(pair + 1) * V7X_LANES] = jnp.where(
            low, outs[2 * pair], outs[2 * pair + 1]).astype(BF16)


def _swa_core_kernel(sink_ref, p_ref, pkv_ref, w_ref, x_ref, o_ref, y_scr):
    blk = SWA_BLOCK
    for i in range(SWA_TILE // blk):
        rows = pl.ds(i * blk, blk)
        k_cur = p_ref[rows, SWA_Q:SWA_Q + SWA_KV]
        v_cur = p_ref[rows, SWA_Q + SWA_KV:]
        if i == 0:
            k_prev, v_prev = pkv_ref[:, :SWA_KV], pkv_ref[:, SWA_KV:]
            first_key = jnp.where(pl.program_id(1) == 0, blk, 0)
        else:
            prev_rows = pl.ds((i - 1) * blk, blk)
            k_prev = p_ref[prev_rows, SWA_Q:SWA_Q + SWA_KV]
            v_prev = p_ref[prev_rows, SWA_Q + SWA_KV:]
            first_key = 0
        _swa_block(sink_ref, p_ref.at[rows, :SWA_Q], k_cur, v_cur, k_prev, v_prev, first_key, y_scr.at[rows, :])
    o_ref[...] = x_ref[...] + _dot(y_scr[...], w_ref[...])


def _swa_core(sinks_perm, qkv, w_out_bf16, x2d, bsz, seq):
    tm = SWA_TILE
    nt = seq // tm
    bpt = tm // SWA_BLOCK
    row_map = lambda b, j: (b * nt + j, 0)
    prev_map = lambda b, j: (jnp.maximum((b * nt + j) * bpt - 1, 0), SWA_Q // (2 * SWA_KV))
    return pl.pallas_call(
        _swa_core_kernel,
        grid=(bsz, nt),
        in_specs=[pl.BlockSpec(memory_space=pltpu.SMEM),
                  pl.BlockSpec((tm, SWA_IN), row_map),
                  pl.BlockSpec((SWA_BLOCK, 2 * SWA_KV), prev_map),
                  pl.BlockSpec((SWA_Q, D_MODEL), lambda b, j: (0, 0)),
                  pl.BlockSpec((tm, D_MODEL), row_map)],
        out_specs=pl.BlockSpec((tm, D_MODEL), row_map),
        out_shape=jax.ShapeDtypeStruct((bsz * seq, D_MODEL), F32),
        scratch_shapes=[pltpu.VMEM((tm, SWA_Q), BF16)],
        compiler_params=_cparams("parallel", "parallel"),
        name="swa_core",
    )(sinks_perm, qkv, qkv, w_out_bf16, x2d)


def _swa_head_order():
    order = []
    for kp in range(SWA_KV_HEADS // 2):
        for i in range(4):
            order += [4 * (2 * kp) + i, 4 * (2 * kp + 1) + i]
    return np.asarray(order)


def _rope_tables(seq):
    inv_freq = ROPE_THETA ** (-np.arange(0, ROT_DIM, 2, dtype=np.float32) / ROT_DIM)
    ang = np.arange(seq, dtype=np.float32)[:, None] * inv_freq[None].astype(np.float32)
    cos, sin = np.cos(ang), np.sin(ang)
    ones = np.ones((seq, SWA_HEAD_DIM - ROT_DIM), np.float32)
    zeros = np.zeros((seq, SWA_HEAD_DIM - ROT_DIM), np.float32)
    z8 = np.zeros((seq, ROT_HALF), np.float32)
    c = np.concatenate([cos, cos, ones], axis=1)
    sa = np.concatenate([-sin, z8, zeros], axis=1)
    sb = np.concatenate([z8, sin, zeros], axis=1)
    tab = np.stack([c, sa, sb])
    return jnp.asarray(np.concatenate([tab, tab], axis=2), F32)


HGRN_WIDTH = 512
HGRN_FD = HGRN_HEADS * HGRN_DK
HGRN_F32_COLS = 2 * HGRN_FD
HGRN_BF16_COLS = 3 * HGRN_FD


def _hgrn_epilogue(c, u_ref, params, outs):
    lb_ref = params[0]
    f32_ref, bf16_ref = outs
    kind, k = divmod(c, HGRN_FD // HGRN_WIDTH)
    cols = slice(k * HGRN_WIDTH, (k + 1) * HGRN_WIDTH)
    u = u_ref[...]
    if kind == 0:
        bf16_ref[:, cols] = (u * (HGRN_DK ** -0.5)).astype(BF16)
    elif kind == 1:
        lb = lb_ref[:, cols]
        f = lb + (1.0 - lb) * _sigmoid(u)
        f32_ref[:, cols] = jnp.log(f)
        f32_ref[:, HGRN_FD + k * HGRN_WIDTH:HGRN_FD + (k + 1) * HGRN_WIDTH] = 1.0 - f
    elif kind == 2:
        bf16_ref[:, HGRN_FD + k * HGRN_WIDTH:HGRN_FD + (k + 1) * HGRN_WIDTH] = u.astype(BF16)
    else:
        bf16_ref[:, 2 * HGRN_FD + k * HGRN_WIDTH:2 * HGRN_FD + (k + 1) * HGRN_WIDTH] = _silu(u).astype(BF16)


def _hgrn_chunk_exact(h, r0, q, v, kk, cum, st_ref, o_scr, sub_row):
    L, C = HGRN_CHUNK, HGRN_SUB
    lanes = slice(h * HGRN_DK, (h + 1) * HGRN_DK)
    total = cum[L - 1:, :]
    st = st_ref[h]
    v_bf = v.astype(BF16)
    o = _dot_nt((q * jnp.exp(cum)).astype(BF16), st.astype(BF16))
    o_sub = []
    for i in range(L // C):
        qi = q[i * C:(i + 1) * C]
        cumi = cum[i * C:(i + 1) * C]
        oi = o[i * C:(i + 1) * C]
        if i > 0:
            ref = cum[i * C - 1:i * C]
            q_t = (qi * jnp.exp(cumi - ref)).astype(BF16)
            k_t = (kk[:i * C] * jnp.exp(ref - cum[:i * C])).astype(BF16)
            attn = _dot_nt(q_t, k_t)
            oi = oi + _dot(attn.astype(BF16), v_bf[:i * C])
        for s in range(C):
            r = i * C + s
            e = jnp.exp(jnp.where(sub_row >= s, cumi - cum[r:r + 1], -jnp.inf))
            w = jnp.sum(qi * kk[r:r + 1] * e, axis=-1, keepdims=True)
            oi = oi + w * v[r:r + 1]
        o_sub.append(oi)
    o_scr[pl.ds(r0, L), lanes] = jnp.concatenate(o_sub, axis=0)
    k_end = (kk * jnp.exp(total - cum)).astype(BF16)
    st_ref[h] = st * jnp.exp(total) + lax.dot_general(
        v_bf, k_end, (((0,), (0,)), ((), ())), preferred_element_type=F32)


def _hgrn_chunk_bounded(r0, qs, vs, kks, cums, st_ref, o_scr, causal):
    L = HGRN_CHUNK
    heads = range(HGRN_HEADS)
    mids = [c[L // 2 - 1:L // 2, :] for c in cums]
    totals = [c[L - 1:, :] for c in cums]
    sts = [st_ref[h] for h in heads]
    v_bf = [v.astype(BF16) for v in vs]
    qe = [qs[h] * jnp.exp(cums[h] - mids[h]) for h in heads]
    ke = [kks[h] * jnp.exp(mids[h] - cums[h]) for h in heads]
    attn = [jnp.where(causal, _dot_nt(qe[h].astype(BF16), ke[h].astype(BF16)), 0.0).astype(BF16) for h in heads]
    o_inter = [_dot_nt((qe[h] * jnp.exp(mids[h])).astype(BF16), sts[h].astype(BF16)) for h in heads]
    o_intra = [_dot(attn[h], v_bf[h]) for h in heads]
    upd = [lax.dot_general(v_bf[h], (ke[h] * jnp.exp(totals[h] - mids[h])).astype(BF16),
                           (((0,), (0,)), ((), ())), preferred_element_type=F32) for h in heads]
    for h in heads:
        o_scr[pl.ds(r0, L), h * HGRN_DK:(h + 1) * HGRN_DK] = o_intra[h] + o_inter[h]
        st_ref[h] = sts[h] * jnp.exp(totals[h]) + upd[h]


def _hgrn_core_kernel(bounded_ref, p_ref, pb_ref, nw_ref, tri_ref, w_ref, x_ref, out_ref, st_ref, o_scr, y_scr, *, tm):
    L, C = HGRN_CHUNK, HGRN_SUB
    fd = HGRN_FD

    @pl.when(pl.program_id(1) == 0)
    def _():
        st_ref[...] = jnp.zeros_like(st_ref)

    sub_row = lax.broadcasted_iota(jnp.int32, (C, HGRN_DK), 0)
    causal = lax.broadcasted_iota(jnp.int32, (L, L), 0) >= lax.broadcasted_iota(jnp.int32, (L, L), 1)
    tri = tri_ref[...]

    def operands(h, r0):
        lanes = slice(h * HGRN_DK, (h + 1) * HGRN_DK)
        q = pb_ref[pl.ds(r0, L), lanes].astype(F32)
        v = pb_ref[pl.ds(r0, L), fd + h * HGRN_DK:fd + (h + 1) * HGRN_DK].astype(F32)
        kk = p_ref[pl.ds(r0, L), fd + h * HGRN_DK:fd + (h + 1) * HGRN_DK]
        cum = _dot_exact_lhs(tri, p_ref[pl.ds(r0, L), lanes])
        return q, v, kk, cum

    def bounded_chunk(c):
        r0 = c * L
        cum_all = _dot_exact_lhs(tri, p_ref[pl.ds(r0, L), :fd])
        qs, vs, kks, cums = [], [], [], []
        for h in range(HGRN_HEADS):
            lanes = slice(h * HGRN_DK, (h + 1) * HGRN_DK)
            qs.append(pb_ref[pl.ds(r0, L), lanes].astype(F32))
            vs.append(pb_ref[pl.ds(r0, L), fd + h * HGRN_DK:fd + (h + 1) * HGRN_DK])
            kks.append(p_ref[pl.ds(r0, L), fd + h * HGRN_DK:fd + (h + 1) * HGRN_DK])
            cums.append(cum_all[:, lanes])
        _hgrn_chunk_bounded(r0, qs, vs, kks, cums, st_ref, o_scr, causal)

    def exact_chunk(c, carry):
        r0 = pl.multiple_of(c * L, L)
        for h in range(HGRN_HEADS):
            _hgrn_chunk_exact(h, r0, *operands(h, r0), st_ref, o_scr, sub_row)
        return carry

    @pl.when(bounded_ref[0] == 1)
    def _():
        for c in range(tm // L):
            bounded_chunk(c)

    @pl.when(bounded_ref[0] == 0)
    def _():
        lax.fori_loop(0, tm // L, exact_chunk, 0)

    for h in range(HGRN_HEADS):
        lanes = slice(h * HGRN_DK, (h + 1) * HGRN_DK)
        o = _rms_rows(o_scr[:, lanes], nw_ref[...])
        y_scr[:, lanes] = (o * pb_ref[:, 2 * fd + h * HGRN_DK:2 * fd + (h + 1) * HGRN_DK].astype(F32)).astype(BF16)
    out_ref[...] = x_ref[...] + _dot(y_scr[...], w_ref[...])


HGRN_MAX_EXPONENT = 80.0


def _hgrn_core(prep_f32, prep_bf16, lower_bound, norm_w, w_out_bf16, x2d, bsz, seq, tm=HGRN_TILE):
    nt = seq // tm
    fd = HGRN_FD
    row_map = lambda b, j: (b * nt + j, 0)
    const2 = lambda b, j: (0, 0)
    tri = jnp.tril(jnp.ones((HGRN_CHUNK, HGRN_CHUNK), F32)).astype(BF16)
    worst = -(HGRN_CHUNK // 2) * jnp.log(jnp.min(lower_bound))
    bounded = (worst <= HGRN_MAX_EXPONENT).astype(jnp.int32).reshape(1)
    return pl.pallas_call(
        functools.partial(_hgrn_core_kernel, tm=tm),
        grid=(bsz, nt),
        in_specs=[pl.BlockSpec(memory_space=pltpu.SMEM),
                  pl.BlockSpec((tm, HGRN_F32_COLS), row_map),
                  pl.BlockSpec((tm, HGRN_BF16_COLS), row_map),
                  pl.BlockSpec((1, HGRN_DK), const2),
                  pl.BlockSpec((HGRN_CHUNK, HGRN_CHUNK), const2),
                  pl.BlockSpec((fd, D_MODEL), const2),
                  pl.BlockSpec((tm, D_MODEL), row_map)],
        out_specs=pl.BlockSpec((tm, D_MODEL), row_map),
        out_shape=jax.ShapeDtypeStruct((bsz * seq, D_MODEL), F32),
        scratch_shapes=[pltpu.VMEM((HGRN_HEADS, HGRN_DK, HGRN_DK), F32),
                        pltpu.VMEM((tm, fd), F32),
                        pltpu.VMEM((tm, fd), BF16)],
        compiler_params=_cparams("parallel", "arbitrary"),
        name="hgrn_core",
    )(bounded, prep_f32, prep_bf16, norm_w.reshape(1, HGRN_DK), tri, w_out_bf16, x2d)


def _ffn_kernel(x_ref, g_ref, wu_ref, cw_ref, cb_ref, wd_ref, o_ref, hn_scr, u_scr, act_scr, prev_scr, *, tm):
    fc = FFN_CHUNK
    first = pl.program_id(1) == 0
    hn_scr[...] = _rms_rows(x_ref[...], g_ref[...]).astype(BF16)

    def up(c):
        for half in range(2):
            cols = slice(half * D_FF + c * fc, half * D_FF + (c + 1) * fc)
            u_scr[c % 2, half, :tm] = _dot(hn_scr[...], wu_ref[:, cols])
            u_scr[c % 2, half, tm:] = jnp.where(first, 0.0, prev_scr[c, half])
            prev_scr[c, half] = u_scr[c % 2, half, tm - CONV_HALO:tm]

    def gate(c):
        u = []
        for half in range(2):
            cols = slice(half * D_FF + c * fc, half * D_FF + (c + 1) * fc)
            u.append(_causal_conv(u_scr[c % 2, half], cw_ref.at[:, cols], cb_ref[:, cols], 3))
        act_scr[:, c * fc:(c + 1) * fc] = (_silu(u[1]) * u[0]).astype(BF16)

    up(0)
    for c in range(N_FFN_CHUNKS):
        if c + 1 < N_FFN_CHUNKS:
            up(c + 1)
        gate(c)
    o_ref[...] = x_ref[...] + _dot(act_scr[...], wd_ref[...])


def _ffn(x2d, gain, w_up_bf16, conv_w, conv_b, w_down_bf16, layer, bsz, seq, tm=512):
    nt = seq // tm
    fc = FFN_CHUNK
    row_map = lambda b, j: (b * nt + j, 0)
    const2 = lambda b, j: (0, 0)
    of_layer = lambda b, j: (layer, 0, 0)
    return pl.pallas_call(
        functools.partial(_ffn_kernel, tm=tm),
        grid=(bsz, nt),
        in_specs=[pl.BlockSpec((tm, D_MODEL), row_map),
                  pl.BlockSpec((1, D_MODEL), const2),
                  pl.BlockSpec((None, D_MODEL, 2 * D_FF), of_layer),
                  pl.BlockSpec((None, 3, 2 * D_FF), of_layer),
                  pl.BlockSpec((None, 1, 2 * D_FF), of_layer),
                  pl.BlockSpec((None, D_FF, D_MODEL), of_layer)],
        out_specs=pl.BlockSpec((tm, D_MODEL), row_map),
        out_shape=jax.ShapeDtypeStruct((bsz * seq, D_MODEL), F32),
        scratch_shapes=[pltpu.VMEM((tm, D_MODEL), BF16),
                        pltpu.VMEM((2, 2, tm + CONV_HALO, fc), F32),
                        pltpu.VMEM((tm, D_FF), BF16),
                        pltpu.VMEM((N_FFN_CHUNKS, 2, CONV_HALO, fc), F32)],
        compiler_params=_cparams("parallel", "arbitrary"),
        name="conv_ffn",
    )(x2d, gain.reshape(1, D_MODEL), w_up_bf16, conv_w, conv_b.reshape(-1, 1, 2 * D_FF), w_down_bf16)


def _lru_mixer(x2d, gain, w_in, conv_w, conv_b, w_gate, b_gate, lam, w_out, bsz, seq):
    coef = (-LRU_C * jax.nn.softplus(-lam.astype(F32))).reshape(1, D_RNN)
    params = [conv_w, conv_b.reshape(1, D_RNN), w_gate.astype(BF16), b_gate.reshape(1, 2 * D_RNN), coef]
    gelu, a, b = _norm_proj(x2d, gain, w_in.astype(BF16), params, [D_RNN] * 3, [BF16, F32, F32], _lru_epilogue,
                            "lru_in", bsz, seq, width=LRU_WIDTH, halo=CONV_HALO)
    return _lru_scan(gelu, a, b, w_out.astype(BF16), x2d, bsz, seq)


def _ssd_mixer(x2d, gain, w_in, conv_w, conv_b, dt_bias, a_log, d_skip, norm_w, w_out, bsz, seq):
    pad = V7X_LANES - SSD_HEADS
    w_dt = jnp.pad(w_in[:, SSD_MAIN:], ((0, 0), (0, pad))).astype(BF16)
    dt_bias_pad = jnp.pad(dt_bias.astype(F32), (0, pad)).reshape(1, V7X_LANES)
    params = [conv_w, conv_b.reshape(1, SSD_CONV_DIM), w_dt, dt_bias_pad]
    zs, xbc, dt = _norm_proj(x2d, gain, w_in[:, :SSD_MAIN].astype(BF16), params,
                             [D_INNER, SSD_CONV_DIM, V7X_LANES], [F32] * 3, _ssd_epilogue, "ssd_in", bsz, seq,
                             width=SSD_WIDTH, halo=CONV_HALO, tail=_ssd_tail)
    a_pad = jnp.pad(-jnp.exp(a_log.astype(F32)), (0, pad)).reshape(1, V7X_LANES)
    d_exp = jnp.repeat(d_skip.astype(F32), SSD_HEAD_DIM).reshape(1, D_INNER)
    return _ssd_core(zs, xbc, dt, a_pad, d_exp, norm_w, w_out.astype(BF16), x2d, bsz, seq)


def _swa_mixer(x2d, gain, w_in, q_norm, k_norm, sinks, w_out, bsz, seq):
    order = _swa_head_order()
    cols = (order[:, None] * SWA_HEAD_DIM + np.arange(SWA_HEAD_DIM)[None]).reshape(-1)
    w_in_perm = jnp.concatenate([w_in[:, cols], w_in[:, SWA_Q:]], axis=1).astype(BF16)
    q_gain = jnp.tile(q_norm.astype(F32), 2).reshape(1, V7X_LANES)
    k_gain = jnp.tile(k_norm.astype(F32), 2).reshape(1, V7X_LANES)
    half = np.arange(V7X_LANES) // SWA_HEAD_DIM
    bd = jnp.asarray(half[:, None] == half[None, :], BF16)
    qkv, = _norm_proj(x2d, gain, w_in_perm, [_rope_tables(seq), q_gain, k_gain, bd], [SWA_IN], [BF16],
                      _swa_epilogue, "swa_in", bsz, seq, width=SWA_WIDTH)
    return _swa_core(sinks.astype(F32)[order], qkv, w_out[cols, :].astype(BF16), x2d, bsz, seq)


def _hgrn_mixer(x2d, gain, w_in, norm_w, w_out, lower_bound, bsz, seq):
    prep_f32, prep_bf16 = _norm_proj(x2d, gain, w_in.astype(BF16), [lower_bound.reshape(1, HGRN_FD)],
                                     [HGRN_F32_COLS, HGRN_BF16_COLS], [F32, BF16], _hgrn_epilogue, "hgrn_in",
                                     bsz, seq, width=HGRN_WIDTH)
    return _hgrn_core(prep_f32, prep_bf16, lower_bound, norm_w, w_out.astype(BF16), x2d, bsz, seq)


def kernel(x, mix_norm, ffn_norm, ffn_w_up, ffn_conv_w, ffn_conv_b, ffn_w_down, lru_w_in, lru_conv_w, lru_conv_b, lru_w_gate, lru_b_gate, lru_lambda, lru_w_out, ssd_w_in, ssd_conv_w, ssd_conv_b, ssd_dt_bias, ssd_a_log, ssd_d, ssd_norm, ssd_w_out, swa_w_in, swa_q_norm, swa_k_norm, swa_sinks, swa_w_out, hgrn_w_in, hgrn_norm, hgrn_w_out, hgrn_lower_bounds):
    bsz, seq, d = x.shape
    depth = mix_norm.shape[0]
    lbs = jnp.cumsum(jax.nn.softmax(hgrn_lower_bounds.astype(F32), axis=0), axis=0)
    lbs = lbs - lbs[0]
    ffn_w_up_bf16 = ffn_w_up.astype(BF16)
    ffn_w_down_bf16 = ffn_w_down.astype(BF16)
    h = x.reshape(bsz * seq, d)
    for i in range(depth):
        kind, j = i % 4, i // 4
        if kind == 0:
            h = _lru_mixer(h, mix_norm[i], lru_w_in[j], lru_conv_w[j], lru_conv_b[j], lru_w_gate[j],
                           lru_b_gate[j], lru_lambda[j], lru_w_out[j], bsz, seq)
        elif kind == 1:
            h = _ssd_mixer(h, mix_norm[i], ssd_w_in[j], ssd_conv_w[j], ssd_conv_b[j], ssd_dt_bias[j],
                           ssd_a_log[j], ssd_d[j], ssd_norm[j], ssd_w_out[j], bsz, seq)
        elif kind == 2:
            h = _swa_mixer(h, mix_norm[i], swa_w_in[j], swa_q_norm[j], swa_k_norm[j], swa_sinks[j],
                           swa_w_out[j], bsz, seq)
        else:
            h = _hgrn_mixer(h, mix_norm[i], hgrn_w_in[j], hgrn_norm[j], hgrn_w_out[j], lbs[i], bsz, seq)
        h = _ffn(h, ffn_norm[i], ffn_w_up_bf16, ffn_conv_w, ffn_conv_b, ffn_w_down_bf16, i, bsz, seq)
    return h.reshape(bsz, seq, d)
```
